```python
import math
import jax, jax.numpy as jnp
from jax import lax
import numpy as np

D_MODEL = 1024
BATCH = 8
SEQ = 4096
DEPTH = 4

N_MEM = 256
D_FF = 2816

DA_HEADS = 4
DA_HEAD_DIM = 64
DA_V_DIM = 2 * DA_HEAD_DIM
DA_QK_WIDTH = DA_HEADS * 2 * DA_HEAD_DIM
DA_WIDTH = DA_HEADS * DA_V_DIM

POOL_WINDOWS = (2, 4, 8, 16)
POOL_GROUPS = 4
POOL_WIDTH = 256
POOL_GROUP_DIM = POOL_WIDTH // POOL_GROUPS

CONV_WIDTH = 256
CONV_KERNEL = 31

XA_HEADS = 4
XA_HEAD_DIM = 64
XA_WIDTH = XA_HEADS * XA_HEAD_DIM

N_BRANCH = 3
Q_BLOCK = 128
EPS = 1e-6
NEG_INF = -1e30

IN_SPLITS = (DA_QK_WIDTH, DA_QK_WIDTH, DA_WIDTH, POOL_WIDTH, 2 * CONV_WIDTH, N_BRANCH * D_MODEL)
IN_COLS = DA_QK_WIDTH * 2 + DA_WIDTH + POOL_WIDTH + 2 * CONV_WIDTH + N_BRANCH * D_MODEL

kernel_name = 'hybrid_gated_diffattn_pool_conv_macaron'


def rmsnorm(x, g):
    xf = x.astype(jnp.float32)
    y = xf * lax.rsqrt(jnp.mean(xf * xf, axis=-1, keepdims=True) + EPS)
    return (y * g.astype(jnp.float32)).astype(x.dtype)


def layernorm(x, g, b):
    xf = x.astype(jnp.float32)
    mu = jnp.mean(xf, axis=-1, keepdims=True)
    xc = xf - mu
    y = xc * lax.rsqrt(jnp.mean(xc * xc, axis=-1, keepdims=True) + EPS)
    return (y * g.astype(jnp.float32) + b.astype(jnp.float32)).astype(x.dtype)


def swiglu_half_step(x, norm_g, w_gu, w_down):
    h = rmsnorm(x, norm_g)
    gate, up = jnp.split(h @ w_gu, 2, axis=-1)
    return x + 0.5 * ((jax.nn.silu(gate) * up) @ w_down)


def split_cols(z):
    idx, acc = [], 0
    for w in IN_SPLITS[:-1]:
        acc += w
        idx.append(acc)
    return jnp.split(z, idx, axis=-1)


def diff_attention(q, k, v, lam, lam_init, subln_g):
    b, s = q.shape[0], q.shape[1]
    nblk = s // Q_BLOCK
    qb = jnp.moveaxis(q.reshape(b, nblk, Q_BLOCK, DA_HEADS, 2, DA_HEAD_DIM), 1, 0)
    kpos = jnp.arange(s)

    def one_block(args):
        i, qi = args
        scores = jnp.einsum('bqhcd,bkhcd->bhcqk', qi, k).astype(jnp.float32)
        qpos = i * Q_BLOCK + jnp.arange(Q_BLOCK)
        causal = kpos[None, :] <= qpos[:, None]
        p = jax.nn.softmax(jnp.where(causal, scores, NEG_INF), axis=-1)
        a = p[:, :, 0] - lam * p[:, :, 1]
        return jnp.einsum('bhqk,bkhe->bqhe', a.astype(v.dtype), v)

    o = lax.map(one_block, (jnp.arange(nblk), qb))
    o = jnp.moveaxis(o, 0, 1).reshape(b, s, DA_HEADS, DA_V_DIM)
    o = rmsnorm(o, subln_g) * (1.0 - lam_init)
    return o.reshape(b, s, DA_WIDTH)


def pool_mixer(u, w_group, scale):
    b, s, _ = u.shape
    uf = u.astype(jnp.float32).reshape(b, s, POOL_GROUPS, POOL_GROUP_DIM)
    c = jnp.cumsum(uf, axis=1)
    c_pad = jnp.pad(c, ((0, 0), (1, 0), (0, 0), (0, 0)))
    pos1 = jnp.arange(1, s + 1, dtype=jnp.float32)
    outs = []
    for g, w in enumerate(POOL_WINDOWS):
        cg = c_pad[:, :, g]
        lag = jnp.pad(cg, ((0, 0), (w, 0), (0, 0)))[:, 1:s + 1]
        win_sum = cg[:, 1:] - lag
        cnt = jnp.minimum(pos1, float(w))
        outs.append(win_sum / cnt[None, :, None] - uf[:, :, g])
    p = jnp.stack(outs, axis=2).astype(u.dtype)
    y = jnp.einsum('bsgc,gcd->bsgd', p, w_group).reshape(b, s, POOL_WIDTH)
    return y * scale


def conv_module(u, dw_w, dw_b, ln_g, ln_b):
    a, gate = jnp.split(u, 2, axis=-1)
    z = a * jax.nn.sigmoid(gate)
    z = lax.conv_general_dilated(
        z, dw_w[:, None, :], window_strides=(1,), padding=[(CONV_KERNEL - 1, 0)],
        dimension_numbers=('NWC', 'WIO', 'NWC'), feature_group_count=CONV_WIDTH) + dw_b
    z = layernorm(z, ln_g, ln_b)
    return jax.nn.silu(z)


def cross_attention_step(x, mem, norm_g, mem_norm_g, w_q, w_kv, q_g, k_g, w_o):
    b, s, _ = x.shape
    m_len = mem.shape[1]
    h = rmsnorm(x, norm_g)
    m = rmsnorm(mem, mem_norm_g)
    q = rmsnorm((h @ w_q).reshape(b, s, XA_HEADS, XA_HEAD_DIM), q_g) * (XA_HEAD_DIM ** -0.5)
    k, v = jnp.split(m @ w_kv, 2, axis=-1)
    k = rmsnorm(k.reshape(b, m_len, XA_HEADS, XA_HEAD_DIM), k_g)
    v = v.reshape(b, m_len, XA_HEADS, XA_HEAD_DIM)
    p = jax.nn.softmax(jnp.einsum('bqhd,bkhd->bhqk', q, k).astype(jnp.float32), axis=-1)
    o = jnp.einsum('bhqk,bkhd->bqhd', p.astype(v.dtype), v).reshape(b, s, XA_WIDTH)
    return x + o @ w_o


def setup_inputs(seed: int = 0) -> dict:
    key = jax.random.key(seed)
    ks = jax.random.split(key, 40)
    L, D = DEPTH, D_MODEL
    f32 = jnp.float32

    def w(k, shape, fan_in):
        return jax.random.normal(k, shape, f32) * (fan_in ** -0.5)

    def gain(k, shape):
        return 1.0 + 0.02 * jax.random.normal(k, shape, f32)

    def bias(k, shape):
        return 0.02 * jax.random.normal(k, shape, f32)

    return {
        'x': jax.random.normal(ks[0], (BATCH, SEQ, D), f32),
        'mem': jax.random.normal(ks[1], (BATCH, N_MEM, D), f32),
        'ffn1_norm': gain(ks[2], (L, D)),
        'ffn1_w_gu': w(ks[3], (L, D, 2 * D_FF), D),
        'ffn1_w_down': w(ks[4], (L, D_FF, D), D_FF),
        'mix_norm': gain(ks[5], (L, D)),
        'w_in': w(ks[6], (L, D, IN_COLS), D),
        'b_gate': bias(ks[7], (L, N_BRANCH * D)),
        'da_q_norm': gain(ks[8], (L, DA_HEAD_DIM)),
        'da_k_norm': gain(ks[9], (L, DA_HEAD_DIM)),
        'da_lambda': 0.1 * jax.random.normal(ks[10], (L, 4, DA_HEAD_DIM), f32),
        'da_subln': gain(ks[11], (L, DA_V_DIM)),
        'w_proj_attn': w(ks[12], (L, DA_WIDTH, D), DA_WIDTH),
        'pool_w': w(ks[13], (L, POOL_GROUPS, POOL_GROUP_DIM, POOL_GROUP_DIM), POOL_GROUP_DIM),
        'pool_scale': gain(ks[14], (L, POOL_WIDTH)),
        'w_proj_pool': w(ks[15], (L, POOL_WIDTH, D), POOL_WIDTH),
        'conv_dw': w(ks[16], (L, CONV_KERNEL, CONV_WIDTH), CONV_KERNEL),
        'conv_db': bias(ks[17], (L, CONV_WIDTH)),
        'conv_ln_g': gain(ks[18], (L, CONV_WIDTH)),
        'conv_ln_b': bias(ks[19], (L, CONV_WIDTH)),
        'w_proj_conv': w(ks[20], (L, CONV_WIDTH, D), CONV_WIDTH),
        'w_out': w(ks[21], (L, D, D), D),
        'xa_norm': gain(ks[22], (L, D)),
        'xa_mem_norm': gain(ks[23], (L, D)),
        'xa_w_q': w(ks[24], (L, D, XA_WIDTH), D),
        'xa_w_kv': w(ks[25], (L, D, 2 * XA_WIDTH), D),
        'xa_q_norm': gain(ks[26], (L, XA_HEAD_DIM)),
        'xa_k_norm': gain(ks[27], (L, XA_HEAD_DIM)),
        'xa_w_o': w(ks[28], (L, XA_WIDTH, D), XA_WIDTH),
        'ffn2_norm': gain(ks[29], (L, D)),
        'ffn2_w_gu': w(ks[30], (L, D, 2 * D_FF), D),
        'ffn2_w_down': w(ks[31], (L, D_FF, D), D_FF),
    }


def reference(x, mem, ffn1_norm, ffn1_w_gu, ffn1_w_down, mix_norm, w_in, b_gate,
              da_q_norm, da_k_norm, da_lambda, da_subln, w_proj_attn,
              pool_w, pool_scale, w_proj_pool,
              conv_dw, conv_db, conv_ln_g, conv_ln_b, w_proj_conv,
              w_out, xa_norm, xa_mem_norm, xa_w_q, xa_w_kv, xa_q_norm, xa_k_norm, xa_w_o,
              ffn2_norm, ffn2_w_gu, ffn2_w_down):
    b, s, _ = x.shape
    for l in range(DEPTH):
        x = swiglu_half_step(x, ffn1_norm[l], ffn1_w_gu[l], ffn1_w_down[l])

        h = rmsnorm(x, mix_norm[l])
        zq, zk, zv, zp, zc, zg = split_cols(h @ w_in[l])

        q = rmsnorm(zq.reshape(b, s, DA_HEADS, 2, DA_HEAD_DIM), da_q_norm[l]) * (DA_HEAD_DIM ** -0.5)
        k = rmsnorm(zk.reshape(b, s, DA_HEADS, 2, DA_HEAD_DIM), da_k_norm[l])
        v = zv.reshape(b, s, DA_HEADS, DA_V_DIM)
        lam_init = 0.8 - 0.6 * math.exp(-0.3 * l)
        lq = da_lambda[l].astype(jnp.float32)
        lam = jnp.exp(jnp.sum(lq[0] * lq[1])) - jnp.exp(jnp.sum(lq[2] * lq[3])) + lam_init
        y_a = diff_attention(q, k, v, lam, lam_init, da_subln[l]) @ w_proj_attn[l]

        y_b = pool_mixer(zp, pool_w[l], pool_scale[l]) @ w_proj_pool[l]

        y_c = conv_module(zc, conv_dw[l], conv_db[l], conv_ln_g[l], conv_ln_b[l]) @ w_proj_conv[l]

        gates = jax.nn.sigmoid((zg + b_gate[l]).astype(jnp.float32)).astype(x.dtype)
        gates = gates.reshape(b, s, N_BRANCH, D_MODEL)
        merged = gates[:, :, 0] * y_a + gates[:, :, 1] * y_b + gates[:, :, 2] * y_c
        x = x + merged @ w_out[l]

        x = cross_attention_step(x, mem, xa_norm[l], xa_mem_norm[l], xa_w_q[l], xa_w_kv[l],
                                 xa_q_norm[l], xa_k_norm[l], xa_w_o[l])

        x = swiglu_half_step(x, ffn2_norm[l], ffn2_w_gu[l], ffn2_w_down[l])
    return x
```

```python
import functools
import math

import jax
import jax.numpy as jnp
from jax import lax
from jax.experimental import pallas as pl
from jax.experimental.pallas import tpu as pltpu

F32 = jnp.float32
BF16 = jnp.bfloat16

EPS = 1e-6
NEG_INF = -1e30

HEAD_DIM = 64
DA_HEADS = 4
DA_V_DIM = 2 * HEAD_DIM
XA_HEADS = 4
POOL_WINDOWS = (2, 4, 8, 16)
CONV_KERNEL = 31
HALO = 32
LANES = 128

VMEM_LIMIT = 56 * 1024 * 1024


def _params(n_grid_dims):
    return pltpu.CompilerParams(
        dimension_semantics=("arbitrary",) * n_grid_dims,
        vmem_limit_bytes=VMEM_LIMIT)


def _resident(shape):
    zeros = (0,) * len(shape)
    return pl.BlockSpec(shape, lambda *_: zeros, pipeline_mode=pl.Buffered(1))


def _rms(xf, g):
    ms = jnp.mean(xf * xf, axis=-1, keepdims=True)
    return xf * lax.rsqrt(ms + EPS) * g


def _group64_rms(z, g):
    width = z.shape[-1]
    lo = lax.broadcasted_iota(jnp.int32, (1, LANES), 1) < HEAD_DIM
    outs = []
    for c in range(width // LANES):
        zc = z[:, c * LANES:(c + 1) * LANES]
        sq = zc * zc
        s_lo = jnp.sum(jnp.where(lo, sq, 0.0), axis=-1, keepdims=True)
        s_hi = jnp.sum(jnp.where(lo, 0.0, sq), axis=-1, keepdims=True)
        ms = jnp.where(lo, s_lo, s_hi) * (1.0 / HEAD_DIM)
        outs.append(zc * lax.rsqrt(ms + EPS))
    return jnp.concatenate(outs, axis=-1) * g


def _dot(a, b):
    return jnp.dot(a, b, preferred_element_type=F32)


def _ffn_kernel(x_ref, g_ref, wgu_ref, wd_ref, o_ref):
    x = x_ref[...]
    h = _rms(x, g_ref[...]).astype(BF16)
    gu = _dot(h, wgu_ref[...])
    d_ff = wd_ref.shape[0]
    gate = gu[:, :d_ff]
    up = gu[:, d_ff:]
    act = (gate * jax.nn.sigmoid(gate) * up).astype(BF16)
    o_ref[...] = x + 0.5 * _dot(act, wd_ref[...])


def _ffn(x2, norm_g, w_gu, w_down, tm=512):
    t, d = x2.shape
    tm = min(tm, t)
    d_ff = w_down.shape[0]
    return pl.pallas_call(
        _ffn_kernel,
        grid=(t // tm,),
        in_specs=[
            pl.BlockSpec((tm, d), lambda i: (i, 0)),
            _resident((1, d)),
            _resident((d, 2 * d_ff)),
            _resident((d_ff, d)),
        ],
        out_specs=pl.BlockSpec((tm, d), lambda i: (i, 0)),
        out_shape=jax.ShapeDtypeStruct((t, d), F32),
        compiler_params=_params(1),
        name="ffn",
    )(x2, norm_g.reshape(1, d), w_gu, w_down)


def _mixer_in_kernel(x_ref, g_ref, w_ref, qg_ref, kg_ref,
                     qt_ref, k_ref, vt_ref, zp_ref, zc_ref):
    h = _rms(x_ref[0], g_ref[...]).astype(BF16)
    z = _dot(h, w_ref[...])
    w_qk = DA_HEADS * 2 * HEAD_DIM
    w_v = DA_HEADS * DA_V_DIM
    pool_w = zp_ref.shape[-1]
    q = _group64_rms(z[:, :w_qk], qg_ref[...]) * (HEAD_DIM ** -0.5)
    k = _group64_rms(z[:, w_qk:2 * w_qk], kg_ref[...])
    v = z[:, 2 * w_qk:2 * w_qk + w_v]
    for hd in range(DA_HEADS):
        sl = slice(hd * DA_V_DIM, (hd + 1) * DA_V_DIM)
        qt_ref[0, hd] = q[:, sl].T.astype(BF16)
        k_ref[0, hd] = k[:, sl].astype(BF16)
        vt_ref[0, hd] = v[:, sl].T.astype(BF16)
    off = 2 * w_qk + w_v
    zp_ref[0] = z[:, off:off + pool_w]
    zc_ref[0] = z[:, off + pool_w:]


def _mixer_in(x, norm_g, w_a, q_g, k_g, pool_width, conv_width, tm=512):
    b, s, d = x.shape
    tm = min(tm, s)
    w_qk = DA_HEADS * 2 * HEAD_DIM
    n_cols = w_a.shape[1]
    tile = lambda g: jnp.tile(g, w_qk // HEAD_DIM).reshape(1, w_qk)
    return pl.pallas_call(
        _mixer_in_kernel,
        grid=(b, s // tm),
        in_specs=[
            pl.BlockSpec((1, tm, d), lambda bi, i: (bi, i, 0)),
            _resident((1, d)),
            _resident((d, n_cols)),
            _resident((1, w_qk)),
            _resident((1, w_qk)),
        ],
        out_specs=[
            pl.BlockSpec((1, DA_HEADS, DA_V_DIM, tm), lambda bi, i: (bi, 0, 0, i)),
            pl.BlockSpec((1, DA_HEADS, tm, DA_V_DIM), lambda bi, i: (bi, 0, i, 0)),
            pl.BlockSpec((1, DA_HEADS, DA_V_DIM, tm), lambda bi, i: (bi, 0, 0, i)),
            pl.BlockSpec((1, tm, pool_width), lambda bi, i: (bi, i, 0)),
            pl.BlockSpec((1, tm, 2 * conv_width), lambda bi, i: (bi, i, 0)),
        ],
        out_shape=[
            jax.ShapeDtypeStruct((b, DA_HEADS, DA_V_DIM, s), BF16),
            jax.ShapeDtypeStruct((b, DA_HEADS, s, DA_V_DIM), BF16),
            jax.ShapeDtypeStruct((b, DA_HEADS, DA_V_DIM, s), BF16),
            jax.ShapeDtypeStruct((b, s, pool_width), F32),
            jax.ShapeDtypeStruct((b, s, 2 * conv_width), F32),
        ],
        compiler_params=_params(2),
        name="mixer_in",
    )(x, norm_g.reshape(1, d), w_a, tile(q_g), tile(k_g))


def _attn_kernel(qt_ref, k_ref, vt_ref, lam_ref, g_ref, o_ref, acc_ref, *, tq, lam_init):
    i = pl.program_id(2)
    qt = qt_ref[0, 0]
    zeros = jnp.zeros((HEAD_DIM, tq), BF16)
    q2 = jnp.concatenate(
        [jnp.concatenate([qt[:HEAD_DIM], zeros], axis=0),
         jnp.concatenate([zeros, qt[HEAD_DIM:]], axis=0)], axis=1)
    acc_ref[...] = jnp.zeros_like(acc_ref)

    def block(j, m, l, diagonal):
        start = pl.multiple_of(j * tq, tq)
        s = _dot(k_ref[0, 0, pl.ds(start, tq), :], q2)
        if diagonal:
            key = lax.broadcasted_iota(jnp.int32, (tq, tq), 0)
            qry = lax.broadcasted_iota(jnp.int32, (tq, tq), 1)
            causal = key <= qry
            causal = jnp.concatenate([causal, causal], axis=1)
            s = jnp.where(causal, s, NEG_INF)
        m_new = jnp.maximum(m, jnp.max(s, axis=0, keepdims=True))
        alpha = jnp.exp(m - m_new)
        p = jnp.exp(s - m_new)
        l_new = alpha * l + jnp.sum(p, axis=0, keepdims=True)
        pv = _dot(vt_ref[0, 0, :, pl.ds(start, tq)], p.astype(BF16))
        acc_ref[...] = alpha * acc_ref[...] + pv
        return m_new, l_new

    m0 = jnp.full((1, 2 * tq), NEG_INF, F32)
    l0 = jnp.zeros((1, 2 * tq), F32)
    m, l = lax.fori_loop(0, i, lambda j, c: block(j, c[0], c[1], False), (m0, l0))
    m, l = block(i, m, l, True)

    lq = lam_ref[...]
    lam = (jnp.exp(jnp.sum(lq[0:1] * lq[1:2], axis=-1, keepdims=True))
           - jnp.exp(jnp.sum(lq[2:3] * lq[3:4], axis=-1, keepdims=True)) + lam_init)
    inv_l = 1.0 / l
    acc = acc_ref[...]
    o = acc[:, :tq] * inv_l[:, :tq] - lam * (acc[:, tq:] * inv_l[:, tq:])
    ms = jnp.mean(o * o, axis=0, keepdims=True)
    o = o * lax.rsqrt(ms + EPS) * g_ref[...] * (1.0 - lam_init)
    o_ref[0] = o.T.astype(BF16)


def _diff_attention(qt, k, vt, da_lambda, subln_g, lam_init, tq=256):
    b, heads, _, s = qt.shape
    tq = min(tq, s)
    kern = functools.partial(_attn_kernel, tq=tq, lam_init=lam_init)
    return pl.pallas_call(
        kern,
        grid=(b, heads, s // tq),
        in_specs=[
            pl.BlockSpec((1, 1, DA_V_DIM, tq), lambda bi, h, i: (bi, h, 0, i)),
            pl.BlockSpec((1, 1, s, DA_V_DIM), lambda bi, h, i: (bi, h, 0, 0)),
            pl.BlockSpec((1, 1, DA_V_DIM, s), lambda bi, h, i: (bi, h, 0, 0)),
            _resident((4, HEAD_DIM)),
            _resident((DA_V_DIM, 1)),
        ],
        out_specs=pl.BlockSpec((1, tq, DA_V_DIM), lambda bi, h, i: (bi, i, h)),
        out_shape=jax.ShapeDtypeStruct((b, s, heads * DA_V_DIM), BF16),
        scratch_shapes=[pltpu.VMEM((DA_V_DIM, 2 * tq), F32)],
        compiler_params=_params(3),
        name="diff_attn",
    )(qt, k, vt, da_lambda, subln_g.reshape(DA_V_DIM, 1))


def _mixer_out_kernel(x_ref, oa_ref, zp_ref, zph_ref, zc_ref, zch_ref,
                      g_ref, wg_ref, bg_ref, pa_ref, wbd_ref, ps_ref, pb_ref,
                      dw_ref, db_ref, lng_ref, lnb_ref, pc_ref, wo_ref,
                      o_ref, extp_ref, extc_ref):
    i = pl.program_id(1)
    ts = x_ref.shape[1]
    d = x_ref.shape[2]
    x = x_ref[0]
    h = _rms(x, g_ref[...]).astype(BF16)
    gates = jax.nn.sigmoid(_dot(h, wg_ref[...]) + bg_ref[...])

    y_a = _dot(oa_ref[0], pa_ref[...])

    keep = (i > 0).astype(F32)
    pool_w = zp_ref.shape[-1]
    u = zp_ref[0]
    extp_ref[0:HALO, :] = zph_ref[0] * keep
    extp_ref[HALO:, :] = u
    lane = lax.broadcasted_iota(jnp.int32, (1, pool_w), 1)
    group_w = pool_w // len(POOL_WINDOWS)
    win = jnp.full((1, pool_w), POOL_WINDOWS[-1], jnp.int32)
    for gi in range(len(POOL_WINDOWS) - 2, -1, -1):
        win = jnp.where(lane < (gi + 1) * group_w, POOL_WINDOWS[gi], win)
    wsum = u
    for dlt in range(1, max(POOL_WINDOWS)):
        wsum = wsum + jnp.where(win > dlt, extp_ref[pl.ds(HALO - dlt, ts), :], 0.0)
    pos1 = (i * ts + 1 + lax.broadcasted_iota(jnp.int32, (ts, 1), 0)).astype(F32)
    cnt = jnp.minimum(pos1, win.astype(F32))
    pooled = wsum / cnt - u
    y_b = _dot(pooled.astype(BF16), wbd_ref[...]) * ps_ref[...]
    y_b = _dot(y_b.astype(BF16), pb_ref[...])

    cw = zc_ref.shape[-1] // 2
    zch = zch_ref[0] * keep
    zc = zc_ref[0]
    extc_ref[0:HALO, :] = zch[:, :cw] * jax.nn.sigmoid(zch[:, cw:])
    extc_ref[HALO:, :] = zc[:, :cw] * jax.nn.sigmoid(zc[:, cw:])
    conv = jnp.zeros((ts, cw), F32) + db_ref[...]
    base = HALO - (CONV_KERNEL - 1)
    for j in range(CONV_KERNEL):
        conv = conv + extc_ref[pl.ds(base + j, ts), :] * dw_ref[j:j + 1, :]
    mu = jnp.mean(conv, axis=-1, keepdims=True)
    xc = conv - mu
    var = jnp.mean(xc * xc, axis=-1, keepdims=True)
    ln = xc * lax.rsqrt(var + EPS) * lng_ref[...] + lnb_ref[...]
    y_c = _dot((ln * jax.nn.sigmoid(ln)).astype(BF16), pc_ref[...])

    merged = gates[:, :d] * y_a + gates[:, d:2 * d] * y_b + gates[:, 2 * d:] * y_c
    o_ref[0] = x + _dot(merged.astype(BF16), wo_ref[...])


def _mixer_out(x, o_attn, zp, zc, norm_g, w_g, b_gate, p_a, w_bd, pool_scale, p_b,
               conv_dw, conv_db, ln_g, ln_b, p_c, w_out, ts=256):
    b, s, d = x.shape
    ts = min(ts, s)
    pool_w = zp.shape[-1]
    cw = zc.shape[-1] // 2
    aw = o_attn.shape[-1]
    per = ts // HALO
    cur = lambda bi, i: (bi, i, 0)
    prev = lambda bi, i: (bi, jnp.maximum(i * per - 1, 0), 0)
    row = lambda a: a.reshape(1, -1)
    return pl.pallas_call(
        _mixer_out_kernel,
        grid=(b, s // ts),
        in_specs=[
            pl.BlockSpec((1, ts, d), cur),
            pl.BlockSpec((1, ts, aw), cur),
            pl.BlockSpec((1, ts, pool_w), cur),
            pl.BlockSpec((1, HALO, pool_w), prev),
            pl.BlockSpec((1, ts, 2 * cw), cur),
            pl.BlockSpec((1, HALO, 2 * cw), prev),
            _resident((1, d)),
            _resident(w_g.shape),
            _resident((1, b_gate.shape[-1])),
            _resident(p_a.shape),
            _resident(w_bd.shape),
            _resident((1, pool_w)),
            _resident(p_b.shape),
            _resident(conv_dw.shape),
            _resident((1, cw)),
            _resident((1, cw)),
            _resident((1, cw)),
            _resident(p_c.shape),
            _resident(w_out.shape),
        ],
        out_specs=pl.BlockSpec((1, ts, d), cur),
        out_shape=jax.ShapeDtypeStruct((b, s, d), F32),
        scratch_shapes=[pltpu.VMEM((HALO + ts, pool_w), F32),
                        pltpu.VMEM((HALO + ts, cw), F32)],
        compiler_params=_params(2),
        name="mixer_out",
    )(x, o_attn, zp, zp, zc, zc, row(norm_g), w_g, row(b_gate), p_a, w_bd,
      row(pool_scale), p_b, conv_dw, row(conv_db), row(ln_g), row(ln_b), p_c, w_out)


def _mem_kv_kernel(mem_ref, g_ref, wkv_ref, kg_ref, kt_ref, v_ref):
    m = _rms(mem_ref[0], g_ref[...]).astype(BF16)
    kv = _dot(m, wkv_ref[...])
    xw = kv.shape[-1] // 2
    k = _group64_rms(kv[:, :xw], kg_ref[...])
    kt_ref[0] = k.T.astype(BF16)
    v_ref[0] = kv[:, xw:].astype(BF16)


def _mem_kv(mem, norm_g, w_kv, k_g):
    b, n_mem, d = mem.shape
    xw = w_kv.shape[1] // 2
    return pl.pallas_call(
        _mem_kv_kernel,
        grid=(b,),
        in_specs=[
            pl.BlockSpec((1, n_mem, d), lambda bi: (bi, 0, 0)),
            _resident((1, d)),
            _resident(w_kv.shape),
            _resident((1, xw)),
        ],
        out_specs=[
            pl.BlockSpec((1, xw, n_mem), lambda bi: (bi, 0, 0)),
            pl.BlockSpec((1, n_mem, xw), lambda bi: (bi, 0, 0)),
        ],
        out_shape=[
            jax.ShapeDtypeStruct((b, xw, n_mem), BF16),
            jax.ShapeDtypeStruct((b, n_mem, xw), BF16),
        ],
        compiler_params=_params(1),
        name="mem_kv",
    )(mem, norm_g.reshape(1, d), w_kv, jnp.tile(k_g, xw // HEAD_DIM).reshape(1, xw))


def _xattn_kernel(x_ref, g_ref, wq_ref, qg_ref, kt_ref, v_ref, wo_ref, o_ref):
    x = x_ref[0]
    h = _rms(x, g_ref[...]).astype(BF16)
    q = _group64_rms(_dot(h, wq_ref[...]), qg_ref[...]) * (HEAD_DIM ** -0.5)
    xw = q.shape[-1]
    lane = lax.broadcasted_iota(jnp.int32, (1, xw), 1)
    kt = kt_ref[0]
    v = v_ref[0]
    o = jnp.zeros(q.shape, F32)
    for hd in range(XA_HEADS):
        sel = (lane >= hd * HEAD_DIM) & (lane < (hd + 1) * HEAD_DIM)
        s = _dot(jnp.where(sel, q, 0.0).astype(BF16), kt)
        p = jnp.exp(s - jnp.max(s, axis=-1, keepdims=True))
        inv_l = 1.0 / jnp.sum(p, axis=-1, keepdims=True)
        o = o + jnp.where(sel, _dot(p.astype(BF16), v) * inv_l, 0.0)
    o_ref[0] = x + _dot(o.astype(BF16), wo_ref[...])


def _xattn(x, kt, v, norm_g, w_q, q_g, w_o, ts=512):
    b, s, d = x.shape
    ts = min(ts, s)
    xw, n_mem = kt.shape[1], kt.shape[2]
    return pl.pallas_call(
        _xattn_kernel,
        grid=(b, s // ts),
        in_specs=[
            pl.BlockSpec((1, ts, d), lambda bi, i: (bi, i, 0)),
            _resident((1, d)),
            _resident(w_q.shape),
            _resident((1, xw)),
            pl.BlockSpec((1, xw, n_mem), lambda bi, i: (bi, 0, 0)),
            pl.BlockSpec((1, n_mem, xw), lambda bi, i: (bi, 0, 0)),
            _resident(w_o.shape),
        ],
        out_specs=pl.BlockSpec((1, ts, d), lambda bi, i: (bi, i, 0)),
        out_shape=jax.ShapeDtypeStruct((b, s, d), F32),
        compiler_params=_params(2),
        name="xattn",
    )(x, norm_g.reshape(1, d), w_q, jnp.tile(q_g, xw // HEAD_DIM).reshape(1, xw), kt, v, w_o)


def _block_diag(w):
    g, c, _ = w.shape
    eye = jnp.eye(g, dtype=w.dtype)
    return (eye[:, None, :, None] * w[:, :, None, :]).reshape(g * c, g * c)


def kernel(x, mem, ffn1_norm, ffn1_w_gu, ffn1_w_down, mix_norm, w_in, b_gate, da_q_norm, da_k_norm, da_lambda, da_subln, w_proj_attn, pool_w, pool_scale, w_proj_pool, conv_dw, conv_db, conv_ln_g, conv_ln_b, w_proj_conv, w_out, xa_norm, xa_mem_norm, xa_w_q, xa_w_kv, xa_q_norm, xa_k_norm, xa_w_o, ffn2_norm, ffn2_w_gu, ffn2_w_down):
    b, s, d = x.shape
    depth = w_in.shape[0]
    pool_width = pool_scale.shape[-1]
    conv_width = conv_db.shape[-1]
    n_branch_in = 3 * DA_HEADS * DA_V_DIM + pool_width + 2 * conv_width
    bf = lambda a: a.astype(BF16)
    for l in range(depth):
        x = _ffn(x.reshape(b * s, d), ffn1_norm[l], bf(ffn1_w_gu[l]), bf(ffn1_w_down[l])).reshape(b, s, d)

        w_in_l = bf(w_in[l])
        qt, k, vt, zp, zc = _mixer_in(x, mix_norm[l], w_in_l[:, :n_branch_in],
                                      da_q_norm[l], da_k_norm[l], pool_width, conv_width)
        lam_init = 0.8 - 0.6 * math.exp(-0.3 * l)
        o_attn = _diff_attention(qt, k, vt, da_lambda[l], da_subln[l], lam_init)
        x = _mixer_out(x, o_attn, zp, zc, mix_norm[l], w_in_l[:, n_branch_in:], b_gate[l],
                       bf(w_proj_attn[l]), bf(_block_diag(pool_w[l])), pool_scale[l], bf(w_proj_pool[l]),
                       conv_dw[l], conv_db[l], conv_ln_g[l], conv_ln_b[l], bf(w_proj_conv[l]), bf(w_out[l]))

        kt, v = _mem_kv(mem, xa_mem_norm[l], bf(xa_w_kv[l]), xa_k_norm[l])
        x = _xattn(x, kt, v, xa_norm[l], bf(xa_w_q[l]), xa_q_norm[l], bf(xa_w_o[l]))

        x = _ffn(x.reshape(b * s, d), ffn2_norm[l], bf(ffn2_w_gu[l]), bf(ffn2_w_down[l])).reshape(b, s, d)
    return x
```

```python
import functools
import math

import jax
import jax.numpy as jnp
from jax import lax
from jax.experimental import pallas as pl
from jax.experimental.pallas import tpu as pltpu

F32 = jnp.float32
BF16 = jnp.bfloat16

EPS = 1e-6
NEG_INF = -1e30
LOG2_E = math.log2(math.e)

HEAD_DIM = 64
DA_HEADS = 4
DA_V_DIM = 2 * HEAD_DIM
XA_HEADS = 4
POOL_WINDOWS = (2, 4, 8, 16)
CONV_KERNEL = 31
HALO = 32
LANES = 128
SUBLANES = 8
MAX_FIXED_OFFSET = 56.0

assert POOL_WINDOWS[0] == 2 and all(b == 2 * a for a, b in zip(POOL_WINDOWS, POOL_WINDOWS[1:]))
assert HALO >= SUBLANES * len(POOL_WINDOWS) and HALO >= CONV_KERNEL - 1 and HALO % SUBLANES == 0

VMEM_LIMIT =56 * 1024 * 1024


def _params(n_grid_dims):
    return pltpu.CompilerParams(
        dimension_semantics=("arbitrary",) * n_grid_dims,
        vmem_limit_bytes=VMEM_LIMIT)


def _resident(shape):
    zeros = (0,) * len(shape)
    return pl.BlockSpec(shape, lambda *_: zeros, pipeline_mode=pl.Buffered(1))


def _rms(xf, g):
    ms = jnp.mean(xf * xf, axis=-1, keepdims=True)
    return xf * lax.rsqrt(ms + EPS) * g


def _group64_rms(z, g):
    width = z.shape[-1]
    lo = lax.broadcasted_iota(jnp.int32, (1, LANES), 1) < HEAD_DIM
    outs = []
    for c in range(width // LANES):
        zc = z[:, c * LANES:(c + 1) * LANES]
        sq = zc * zc
        s_lo = jnp.sum(jnp.where(lo, sq, 0.0), axis=-1, keepdims=True)
        s_hi = jnp.sum(jnp.where(lo, 0.0, sq), axis=-1, keepdims=True)
        ms = jnp.where(lo, s_lo, s_hi) * (1.0 / HEAD_DIM)
        outs.append(zc * lax.rsqrt(ms + EPS))
    return jnp.concatenate(outs, axis=-1) * g


def _sigmoid(x):
    return 0.5 * jnp.tanh(0.5 * x) + 0.5


def _dot(a, b):
    return jnp.dot(a, b, preferred_element_type=F32)


def _ffn_kernel(x_ref, g_ref, wgu_ref, wd_ref, o_ref):
    x = x_ref[...]
    h = _rms(x, g_ref[...]).astype(BF16)
    gu = _dot(h, wgu_ref[...])
    d_ff = wd_ref.shape[0]
    gate = gu[:, :d_ff]
    up = gu[:, d_ff:]
    act = (gate * _sigmoid(gate) * up).astype(BF16)
    o_ref[...] = x + 0.5 * _dot(act, wd_ref[...])


def _ffn(x2, norm_g, w_gu, w_down, tm=512):
    t, d = x2.shape
    tm = min(tm, t)
    d_ff = w_down.shape[0]
    return pl.pallas_call(
        _ffn_kernel,
        grid=(t // tm,),
        in_specs=[
            pl.BlockSpec((tm, d), lambda i: (i, 0)),
            _resident((1, d)),
            _resident((d, 2 * d_ff)),
            _resident((d_ff, d)),
        ],
        out_specs=pl.BlockSpec((tm, d), lambda i: (i, 0)),
        out_shape=jax.ShapeDtypeStruct((t, d), F32),
        compiler_params=_params(1),
        name="ffn",
    )(x2, norm_g.reshape(1, d), w_gu, w_down)


def _mixer_in_kernel(x_ref, g_ref, w_ref, qg_ref, kg_ref,
                     qt_ref, k_ref, vt_ref, zp_ref, zc_ref):
    h = _rms(x_ref[0], g_ref[...]).astype(BF16)
    z = _dot(h, w_ref[...])
    w_qk = DA_HEADS * 2 * HEAD_DIM
    w_v = DA_HEADS * DA_V_DIM
    pool_w = zp_ref.shape[-1]
    q = _group64_rms(z[:, :w_qk], qg_ref[...]) * (HEAD_DIM ** -0.5 * LOG2_E)
    k = _group64_rms(z[:, w_qk:2 * w_qk], kg_ref[...])
    v = z[:, 2 * w_qk:2 * w_qk + w_v]
    for hd in range(DA_HEADS):
        sl = slice(hd * DA_V_DIM, (hd + 1) * DA_V_DIM)
        qt_ref[0, hd] = q[:, sl].T.astype(BF16)
        k_ref[0, hd] = k[:, sl].astype(BF16)
        vt_ref[0, hd] = v[:, sl].T.astype(BF16)
    off = 2 * w_qk + w_v
    zp_ref[0] = z[:, off:off + pool_w]
    zc_ref[0] = z[:, off + pool_w:]


def _mixer_in(x, norm_g, w_a, q_g, k_g, pool_width, conv_width, tm=512):
    b, s, d = x.shape
    tm = min(tm, s)
    w_qk = DA_HEADS * 2 * HEAD_DIM
    n_cols = w_a.shape[1]
    tile = lambda g: jnp.tile(g, w_qk // HEAD_DIM).reshape(1, w_qk)
    return pl.pallas_call(
        _mixer_in_kernel,
        grid=(b, s // tm),
        in_specs=[
            pl.BlockSpec((1, tm, d), lambda bi, i: (bi, i, 0)),
            _resident((1, d)),
            _resident((d, n_cols)),
            _resident((1, w_qk)),
            _resident((1, w_qk)),
        ],
        out_specs=[
            pl.BlockSpec((1, DA_HEADS, DA_V_DIM, tm), lambda bi, i: (bi, 0, 0, i)),
            pl.BlockSpec((1, DA_HEADS, tm, DA_V_DIM), lambda bi, i: (bi, 0, i, 0)),
            pl.BlockSpec((1, DA_HEADS, DA_V_DIM, tm), lambda bi, i: (bi, 0, 0, i)),
            pl.BlockSpec((1, tm, pool_width), lambda bi, i: (bi, i, 0)),
            pl.BlockSpec((1, tm, 2 * conv_width), lambda bi, i: (bi, i, 0)),
        ],
        out_shape=[
            jax.ShapeDtypeStruct((b, DA_HEADS, DA_V_DIM, s), BF16),
            jax.ShapeDtypeStruct((b, DA_HEADS, s, DA_V_DIM), BF16),
            jax.ShapeDtypeStruct((b, DA_HEADS, DA_V_DIM, s), BF16),
            jax.ShapeDtypeStruct((b, s, pool_width), F32),
            jax.ShapeDtypeStruct((b, s, 2 * conv_width), F32),
        ],
        compiler_params=_params(2),
        name="mixer_in",
    )(x, norm_g.reshape(1, d), w_a, tile(q_g), tile(k_g))


def _attn_kernel(bound_ref, qt_ref, k_ref, vt_ref, lam_ref, g_ref, o_ref,
                 q2_ref, m_ref, l_ref, acc_ref, *, tq, lam_init):
    i = pl.program_id(1)
    heads = qt_ref.shape[1]
    sub = l_ref.shape[1]
    q2_ref[...] = jnp.zeros_like(q2_ref)
    for hd in range(heads):
        q2_ref[hd, :HEAD_DIM, :tq] = qt_ref[0, hd, :HEAD_DIM, :]
        q2_ref[hd, HEAD_DIM:, tq:] = qt_ref[0, hd, HEAD_DIM:, :]
    l_ref[...] = jnp.zeros_like(l_ref)
    acc_ref[...] = jnp.zeros_like(acc_ref)

    def scores(j, hd, diagonal):
        start = pl.multiple_of(j * tq, tq)
        s = _dot(k_ref[0, hd, pl.ds(start, tq), :], q2_ref[hd])
        if diagonal:
            key = lax.broadcasted_iota(jnp.int32, (tq, tq), 0)
            qry = lax.broadcasted_iota(jnp.int32, (tq, tq), 1)
            causal = key <= qry
            s = jnp.where(jnp.concatenate([causal, causal], axis=1), s, NEG_INF)
        return s, vt_ref[0, hd, :, pl.ds(start, tq)]

    def block_fixed(j, diagonal):
        c = bound_ref[0]
        nxt = scores(j, 0, diagonal)
        for hd in range(heads):
            s, vt = nxt
            if hd + 1 < heads:
                nxt = scores(j, hd + 1, diagonal)
            p = jnp.exp2(s - c)
            l_ref[hd] += jnp.sum(p.reshape(tq // sub, sub, 2 * tq), axis=0)
            acc_ref[hd] += _dot(vt, p.astype(BF16))

    def block_running(j, diagonal):
        for hd in range(heads):
            s, vt = scores(j, hd, diagonal)
            m_old = m_ref[hd]
            m_new = jnp.maximum(m_old, jnp.max(s, axis=0, keepdims=True))
            alpha = jnp.exp2(m_old - m_new)
            p = jnp.exp2(s - m_new)
            m_ref[hd] = m_new
            l_ref[hd, 0:1, :] = alpha * l_ref[hd, 0:1, :] + jnp.sum(p, axis=0, keepdims=True)
            acc_ref[hd] = alpha * acc_ref[hd] + _dot(vt, p.astype(BF16))

    def run(block):
        def body(j, carry):
            block(j, False)
            return carry
        lax.fori_loop(0, i, body, 0)
        block(i, True)

    fixed_ok = bound_ref[0] <= MAX_FIXED_OFFSET

    @pl.when(fixed_ok)
    def _():
        run(block_fixed)

    @pl.when(jnp.logical_not(fixed_ok))
    def _():
        m_ref[...] = jnp.full_like(m_ref, NEG_INF)
        run(block_running)

    lq = lam_ref[...]
    lam = (jnp.exp(jnp.sum(lq[0:1] * lq[1:2], axis=-1, keepdims=True))
           - jnp.exp(jnp.sum(lq[2:3] * lq[3:4], axis=-1, keepdims=True)) + lam_init)
    for hd in range(heads):
        inv_l = 1.0 / jnp.sum(l_ref[hd], axis=0, keepdims=True)
        acc = acc_ref[hd]
        o = acc[:, :tq] * inv_l[:, :tq] - lam * (acc[:, tq:] * inv_l[:, tq:])
        ms = jnp.mean(o * o, axis=0, keepdims=True)
        o = o * lax.rsqrt(ms + EPS) * g_ref[...] * (1.0 - lam_init)
        o_ref[0, :, hd * DA_V_DIM:(hd + 1) * DA_V_DIM] = o.T.astype(BF16)


def _score_bound(q_g, k_g):
    bound = (HEAD_DIM * HEAD_DIM ** -0.5 * LOG2_E) * jnp.max(jnp.abs(q_g)) * jnp.max(jnp.abs(k_g))
    return (1.02 * bound).reshape(1).astype(F32)


def _diff_attention(qt, k, vt, bound, da_lambda, subln_g, lam_init, tq=512):
    b, heads, _, s = qt.shape
    tq = min(tq, s)
    kern = functools.partial(_attn_kernel, tq=tq, lam_init=lam_init)
    return pl.pallas_call(
        kern,
        grid=(b, s // tq),
        in_specs=[
            pl.BlockSpec(memory_space=pltpu.SMEM),
            pl.BlockSpec((1, heads, DA_V_DIM, tq), lambda bi, i: (bi, 0, 0, i)),
            pl.BlockSpec((1, heads, s, DA_V_DIM), lambda bi, i: (bi, 0, 0, 0)),
            pl.BlockSpec((1, heads, DA_V_DIM, s), lambda bi, i: (bi, 0, 0, 0)),
            _resident((4, HEAD_DIM)),
            _resident((DA_V_DIM, 1)),
        ],
        out_specs=pl.BlockSpec((1, tq, heads * DA_V_DIM), lambda bi, i: (bi, i, 0)),
        out_shape=jax.ShapeDtypeStruct((b, s, heads * DA_V_DIM), BF16),
        scratch_shapes=[pltpu.VMEM((heads, DA_V_DIM, 2 * tq), BF16),
                        pltpu.VMEM((heads, 1, 2 * tq), F32),
                        pltpu.VMEM((heads, SUBLANES, 2 * tq), F32),
                        pltpu.VMEM((heads, DA_V_DIM, 2 * tq), F32)],
        compiler_params=_params(2),
        name="diff_attn",
    )(bound, qt, k, vt, da_lambda, subln_g.reshape(DA_V_DIM, 1))


def _mixer_out_kernel(x_ref, oa_ref, zp_ref, zph_ref, zc_ref, zch_ref,
                      g_ref, wg_ref, bg_ref, pa_ref, wbd_ref, ps_ref, pb_ref,
                      dw_ref, db_ref, lng_ref, lnb_ref, pc_ref, wo_ref,
                      o_ref, extp_ref, sums_ref, extc_ref, shift_ref):
    i = pl.program_id(1)
    ts = x_ref.shape[1]
    d = x_ref.shape[2]
    x = x_ref[0]
    h = _rms(x, g_ref[...]).astype(BF16)
    gates = _sigmoid(_dot(h, wg_ref[...]) + bg_ref[...])

    y_a = _dot(oa_ref[0], pa_ref[...])

    keep = (i > 0).astype(F32)
    pool_w = zp_ref.shape[-1]
    u = zp_ref[0]
    extp_ref[0:HALO, :] = zph_ref[0] * keep
    extp_ref[HALO:, :] = u
    lane = lax.broadcasted_iota(jnp.int32, (1, pool_w), 1)
    group_w = pool_w // len(POOL_WINDOWS)
    win = jnp.full((1, pool_w), POOL_WINDOWS[-1], jnp.int32)
    for gi in range(len(POOL_WINDOWS) - 2, -1, -1):
        win = jnp.where(lane < (gi + 1) * group_w, POOL_WINDOWS[gi], win)
    ext = HALO + ts
    prev_ref, lo = extp_ref, 0
    sums = []
    for lvl, w in enumerate(POOL_WINDOWS):
        half = w // 2
        if lvl + 1 < len(POOL_WINDOWS):
            lo += SUBLANES
            sums_ref[lvl, lo:ext, :] = prev_ref[lo:ext, :] + prev_ref[pl.ds(lo - half, ext - lo), :]
            prev_ref = sums_ref.at[lvl]
            sums.append(prev_ref[HALO:ext, :])
        else:
            sums.append(prev_ref[HALO:ext, :] + prev_ref[pl.ds(HALO - half, ts), :])
    wsum = sums[-1]
    for lvl in range(len(POOL_WINDOWS) - 2, -1, -1):
        wsum = jnp.where(win == POOL_WINDOWS[lvl], sums[lvl], wsum)
    pos1 =(i * ts + 1 + lax.broadcasted_iota(jnp.int32, (ts, 1), 0)).astype(F32)
    cnt = jnp.minimum(pos1, win.astype(F32))
    pooled = wsum / cnt - u
    y_b = _dot(pooled.astype(BF16), wbd_ref[...]) * ps_ref[...]
    y_b = _dot(y_b.astype(BF16), pb_ref[...])

    cw = zc_ref.shape[-1] // 2
    zch = zch_ref[0] * keep
    zc = zc_ref[0]
    extc_ref[0:HALO, :] = zch[:, :cw] * _sigmoid(zch[:, cw:])
    extc_ref[HALO:, :] = zc[:, :cw] * _sigmoid(zc[:, cw:])
    for r in range(1, SUBLANES):
        shift_ref[r - 1] = extc_ref[pl.ds(r, ext - SUBLANES), :]
    conv = jnp.zeros((ts, cw), F32) + db_ref[...]
    base = HALO - (CONV_KERNEL - 1)
    for j in range(CONV_KERNEL):
        q8, r = divmod(base + j, SUBLANES)
        src = extc_ref if r == 0 else shift_ref.at[r - 1]
        conv = conv + src[q8 * SUBLANES:q8 * SUBLANES + ts, :] * dw_ref[j:j + 1, :]
    mu = jnp.mean(conv, axis=-1, keepdims=True)
    xc = conv - mu
    var = jnp.mean(xc * xc, axis=-1, keepdims=True)
    ln = xc * lax.rsqrt(var + EPS) * lng_ref[...] + lnb_ref[...]
    y_c = _dot((ln * _sigmoid(ln)).astype(BF16), pc_ref[...])

    merged = gates[:, :d] * y_a + gates[:, d:2 * d] * y_b + gates[:, 2 * d:] * y_c
    o_ref[0] = x + _dot(merged.astype(BF16), wo_ref[...])


def _mixer_out(x, o_attn, zp, zc, norm_g, w_g, b_gate, p_a, w_bd, pool_scale, p_b,
               conv_dw, conv_db, ln_g, ln_b, p_c, w_out, ts=512):
    b, s, d = x.shape
    ts = min(ts, s)
    pool_w = zp.shape[-1]
    cw = zc.shape[-1] // 2
    aw = o_attn.shape[-1]
    per = ts // HALO
    cur = lambda bi, i: (bi, i, 0)
    prev = lambda bi, i: (bi, jnp.maximum(i * per - 1, 0), 0)
    row = lambda a: a.reshape(1, -1)
    return pl.pallas_call(
        _mixer_out_kernel,
        grid=(b, s // ts),
        in_specs=[
            pl.BlockSpec((1, ts, d), cur),
            pl.BlockSpec((1, ts, aw), cur),
            pl.BlockSpec((1, ts, pool_w), cur),
            pl.BlockSpec((1, HALO, pool_w), prev),
            pl.BlockSpec((1, ts, 2 * cw), cur),
            pl.BlockSpec((1, HALO, 2 * cw), prev),
            _resident((1, d)),
            _resident(w_g.shape),
            _resident((1, b_gate.shape[-1])),
            _resident(p_a.shape),
            _resident(w_bd.shape),
            _resident((1, pool_w)),
            _resident(p_b.shape),
            _resident(conv_dw.shape),
            _resident((1, cw)),
            _resident((1, cw)),
            _resident((1, cw)),
            _resident(p_c.shape),
            _resident(w_out.shape),
        ],
        out_specs=pl.BlockSpec((1, ts, d), cur),
        out_shape=jax.ShapeDtypeStruct((b, s, d), F32),
        scratch_shapes=[pltpu.VMEM((HALO + ts, pool_w), F32),
                        pltpu.VMEM((len(POOL_WINDOWS) - 1, HALO + ts, pool_w), F32),
                        pltpu.VMEM((HALO + ts, cw), F32),
                        pltpu.VMEM((SUBLANES - 1, HALO + ts - SUBLANES, cw), F32)],
        compiler_params=_params(2),
        name="mixer_out",
    )(x, o_attn, zp, zp, zc, zc, row(norm_g), w_g, row(b_gate), p_a, w_bd,
      row(pool_scale), p_b, conv_dw, row(conv_db), row(ln_g), row(ln_b), p_c, w_out)


def _mem_kv_kernel(mem_ref, g_ref, wkv_ref, kg_ref, kt_ref, v_ref):
    m = _rms(mem_ref[0], g_ref[...]).astype(BF16)
    kv = _dot(m, wkv_ref[...])
    xw = kv.shape[-1] // 2
    k = _group64_rms(kv[:, :xw], kg_ref[...])
    kt_ref[0] = k.T.astype(BF16)
    v_ref[0] = kv[:, xw:].astype(BF16)


def _mem_kv(mem, norm_g, w_kv, k_g):
    b, n_mem, d = mem.shape
    xw = w_kv.shape[1] // 2
    return pl.pallas_call(
        _mem_kv_kernel,
        grid=(b,),
        in_specs=[
            pl.BlockSpec((1, n_mem, d), lambda bi: (bi, 0, 0)),
            _resident((1, d)),
            _resident(w_kv.shape),
            _resident((1, xw)),
        ],
        out_specs=[
            pl.BlockSpec((1, xw, n_mem), lambda bi: (bi, 0, 0)),
            pl.BlockSpec((1, n_mem, xw), lambda bi: (bi, 0, 0)),
        ],
        out_shape=[
            jax.ShapeDtypeStruct((b, xw, n_mem), BF16),
            jax.ShapeDtypeStruct((b, n_mem, xw), BF16),
        ],
        compiler_params=_params(1),
        name="mem_kv",
    )(mem, norm_g.reshape(1, d), w_kv, jnp.tile(k_g, xw // HEAD_DIM).reshape(1, xw))


def _xattn_kernel(x_ref, g_ref, wq_ref, qg_ref, kt_ref, v_ref, wo_ref, o_ref):
    x = x_ref[0]
    h = _rms(x, g_ref[...]).astype(BF16)
    q = _group64_rms(_dot(h, wq_ref[...]), qg_ref[...]) * (HEAD_DIM ** -0.5)
    xw = q.shape[-1]
    lane = lax.broadcasted_iota(jnp.int32, (1, xw), 1)
    kt = kt_ref[0]
    v = v_ref[0]
    o = jnp.zeros(q.shape, F32)
    for hd in range(XA_HEADS):
        sel = (lane >= hd * HEAD_DIM) & (lane < (hd + 1) * HEAD_DIM)
        s = _dot(jnp.where(sel, q, 0.0).astype(BF16), kt)
        p = jnp.exp(s - jnp.max(s, axis=-1, keepdims=True))
        inv_l = 1.0 / jnp.sum(p, axis=-1, keepdims=True)
        o = o + jnp.where(sel, _dot(p.astype(BF16), v) * inv_l, 0.0)
    o_ref[0] = x + _dot(o.astype(BF16), wo_ref[...])


def _xattn(x, kt, v, norm_g, w_q, q_g, w_o, ts=512):
    b, s, d = x.shape
    ts = min(ts, s)
    xw, n_mem = kt.shape[1], kt.shape[2]
    return pl.pallas_call(
        _xattn_kernel,
        grid=(b, s // ts),
        in_specs=[
            pl.BlockSpec((1, ts, d), lambda bi, i: (bi, i, 0)),
            _resident((1, d)),
            _resident(w_q.shape),
            _resident((1, xw)),
            pl.BlockSpec((1, xw, n_mem), lambda bi, i: (bi, 0, 0)),
            pl.BlockSpec((1, n_mem, xw), lambda bi, i: (bi, 0, 0)),
            _resident(w_o.shape),
        ],
        out_specs=pl.BlockSpec((1, ts, d), lambda bi, i: (bi, i, 0)),
        out_shape=jax.ShapeDtypeStruct((b, s, d), F32),
        compiler_params=_params(2),
        name="xattn",
    )(x, norm_g.reshape(1, d), w_q, jnp.tile(q_g, xw // HEAD_DIM).reshape(1, xw), kt, v, w_o)


def _block_diag(w):
    g, c, _ = w.shape
    eye = jnp.eye(g, dtype=w.dtype)
    return (eye[:, None, :, None] * w[:, :, None, :]).reshape(g * c, g * c)


def kernel(x, mem, ffn1_norm, ffn1_w_gu, ffn1_w_down, mix_norm, w_in, b_gate, da_q_norm, da_k_norm, da_lambda, da_subln, w_proj_attn, pool_w, pool_scale, w_proj_pool, conv_dw, conv_db, conv_ln_g, conv_ln_b, w_proj_conv, w_out, xa_norm, xa_mem_norm, xa_w_q, xa_w_kv, xa_q_norm, xa_k_norm, xa_w_o, ffn2_norm, ffn2_w_gu, ffn2_w_down):
    b, s, d = x.shape
    depth = w_in.shape[0]
    pool_width = pool_scale.shape[-1]
    conv_width = conv_db.shape[-1]
    n_branch_in = 3 * DA_HEADS * DA_V_DIM + pool_width + 2 * conv_width
    bf = lambda a: a.astype(BF16)
    for l in range(depth):
        x = _ffn(x.reshape(b * s, d), ffn1_norm[l], bf(ffn1_w_gu[l]), bf(ffn1_w_down[l])).reshape(b, s, d)

        w_in_l = bf(w_in[l])
        qt, k, vt, zp, zc = _mixer_in(x, mix_norm[l], w_in_l[:, :n_branch_in],
                                      da_q_norm[l], da_k_norm[l], pool_width, conv_width)
        lam_init = 0.8 - 0.6 * math.exp(-0.3 * l)
        o_attn = _diff_attention(qt, k, vt, _score_bound(da_q_norm[l], da_k_norm[l]),
                                 da_lambda[l], da_subln[l], lam_init)
        x = _mixer_out(x, o_attn, zp, zc, mix_norm[l], w_in_l[:, n_branch_in:], b_gate[l],
                       bf(w_proj_attn[l]), bf(_block_diag(pool_w[l])), pool_scale[l], bf(w_proj_pool[l]),
                       conv_dw[l], conv_db[l], conv_ln_g[l], conv_ln_b[l], bf(w_proj_conv[l]), bf(w_out[l]))

        kt, v = _mem_kv(mem, xa_mem_norm[l], bf(xa_w_kv[l]), xa_k_norm[l])
        x = _xattn(x, kt, v, xa_norm[l], bf(xa_w_q[l]), xa_q_norm[l], bf(xa_w_o[l]))

        x = _ffn(x.reshape(b * s, d), ffn2_norm[l], bf(ffn2_w_gu[l]), bf(ffn2_w_down[l])).reshape(b, s, d)
    return x
```

```python
import functools
import math

import jax
import jax.numpy as jnp
from jax import lax
from jax.experimental import pallas as pl
from jax.experimental.pallas import tpu as pltpu

F32 = jnp.float32
BF16 = jnp.bfloat16

EPS = 1e-6
NEG_INF = -1e30
LOG2_E = math.log2(math.e)

HEAD_DIM = 64
DA_HEADS = 4
DA_V_DIM = 2 * HEAD_DIM
XA_HEADS = 4
POOL_WINDOWS = (2, 4, 8, 16)
CONV_KERNEL = 31
HALO = 32
LANES = 128
SUBLANES = 8
MAX_FIXED_OFFSET = 56.0

assert POOL_WINDOWS[0] == 2 and all(b == 2 * a for a, b in zip(POOL_WINDOWS, POOL_WINDOWS[1:]))
assert HALO >= SUBLANES * len(POOL_WINDOWS) and HALO >= CONV_KERNEL - 1 and HALO % SUBLANES == 0

VMEM_LIMIT = 56 * 1024 * 1024


def _params(n_grid_dims):
    return pltpu.CompilerParams(
        dimension_semantics=("arbitrary",) * n_grid_dims,
        vmem_limit_bytes=VMEM_LIMIT)


def _layer(arr, l):
    zeros = (0,) * (arr.ndim - 1)
    return pl.BlockSpec((None,) + arr.shape[1:], lambda *_: (l,) + zeros,
                        pipeline_mode=pl.Buffered(1))


def _rows(a):
    return a.reshape(a.shape[0], 1, -1)


def _rms(xf, g):
    ms = jnp.mean(xf * xf, axis=-1, keepdims=True)
    return xf * lax.rsqrt(ms + EPS) * g


def _group64_rms(z, g):
    width = z.shape[-1]
    lo = lax.broadcasted_iota(jnp.int32, (1, LANES), 1) < HEAD_DIM
    outs = []
    for c in range(width // LANES):
        zc = z[:, c * LANES:(c + 1) * LANES]
        sq = zc * zc
        s_lo = jnp.sum(jnp.where(lo, sq, 0.0), axis=-1, keepdims=True)
        s_hi = jnp.sum(jnp.where(lo, 0.0, sq), axis=-1, keepdims=True)
        ms = jnp.where(lo, s_lo, s_hi) * (1.0 / HEAD_DIM)
        outs.append(zc * lax.rsqrt(ms + EPS))
    return jnp.concatenate(outs, axis=-1) * g


def _sigmoid(x):
    return 0.5 * jnp.tanh(0.5 * x) + 0.5


def _dot(a, b):
    return jnp.dot(a, b, preferred_element_type=F32)


def _pipelined(produce, consume, n):
    nxt = produce(0)
    for t in range(n):
        cur = nxt
        if t + 1 < n:
            nxt = produce(t + 1)
        consume(t, cur)


def _ffn_kernel(x_ref, g_ref, wgu_ref, wd_ref, o_ref):
    x = x_ref[...]
    h = _rms(x, g_ref[...]).astype(BF16)
    gu = _dot(h, wgu_ref[...])
    d_ff = wd_ref.shape[0]
    gate = gu[:, :d_ff]
    up = gu[:, d_ff:]
    act = (gate * _sigmoid(gate) * up).astype(BF16)
    o_ref[...] = x + 0.5 * _dot(act, wd_ref[...])


def _ffn(x2, l, norm_g, w_gu, w_down, tm=512):
    t, d = x2.shape
    tm = min(tm, t)
    return pl.pallas_call(
        _ffn_kernel,
        grid=(t // tm,),
        in_specs=[
            pl.BlockSpec((tm, d), lambda i: (i, 0)),
            _layer(norm_g, l),
            _layer(w_gu, l),
            _layer(w_down, l),
        ],
        out_specs=pl.BlockSpec((tm, d), lambda i: (i, 0)),
        out_shape=jax.ShapeDtypeStruct((t, d), F32),
        compiler_params=_params(1),
        name="ffn",
    )(x2, norm_g, w_gu, w_down)


def _mixer_in_kernel(x_ref, g_ref, w_ref, qg_ref, kg_ref,
                     qt_ref, k_ref, vt_ref, zp_ref, zc_ref):
    h = _rms(x_ref[0], g_ref[...]).astype(BF16)
    z = _dot(h, w_ref[...])
    w_qk = DA_HEADS * 2 * HEAD_DIM
    w_v = DA_HEADS * DA_V_DIM
    pool_w = zp_ref.shape[-1]
    q = _group64_rms(z[:, :w_qk], qg_ref[...]) * (HEAD_DIM ** -0.5 * LOG2_E)
    k = _group64_rms(z[:, w_qk:2 * w_qk], kg_ref[...])
    v = z[:, 2 * w_qk:2 * w_qk + w_v]
    for hd in range(DA_HEADS):
        sl = slice(hd * DA_V_DIM, (hd + 1) * DA_V_DIM)
        qt_ref[0, hd] = q[:, sl].T.astype(BF16)
        k_ref[0, hd] = k[:, sl].astype(BF16)
        vt_ref[0, hd] = v[:, sl].T.astype(BF16)
    off = 2 * w_qk + w_v
    zp_ref[0] = z[:, off:off + pool_w]
    zc_ref[0] = z[:, off + pool_w:]


def _mixer_in(x, l, norm_g, w_a, q_g, k_g, pool_width, conv_width, tm=512):
    b, s, d = x.shape
    tm = min(tm, s)
    return pl.pallas_call(
        _mixer_in_kernel,
        grid=(b, s // tm),
        in_specs=[
            pl.BlockSpec((1, tm, d), lambda bi, i: (bi, i, 0)),
            _layer(norm_g, l),
            _layer(w_a, l),
            _layer(q_g, l),
            _layer(k_g, l),
        ],
        out_specs=[
            pl.BlockSpec((1, DA_HEADS, DA_V_DIM, tm), lambda bi, i: (bi, 0, 0, i)),
            pl.BlockSpec((1, DA_HEADS, tm, DA_V_DIM), lambda bi, i: (bi, 0, i, 0)),
            pl.BlockSpec((1, DA_HEADS, DA_V_DIM, tm), lambda bi, i: (bi, 0, 0, i)),
            pl.BlockSpec((1, tm, pool_width), lambda bi, i: (bi, i, 0)),
            pl.BlockSpec((1, tm, 2 * conv_width), lambda bi, i: (bi, i, 0)),
        ],
        out_shape=[
            jax.ShapeDtypeStruct((b, DA_HEADS, DA_V_DIM, s), BF16),
            jax.ShapeDtypeStruct((b, DA_HEADS, s, DA_V_DIM), BF16),
            jax.ShapeDtypeStruct((b, DA_HEADS, DA_V_DIM, s), BF16),
            jax.ShapeDtypeStruct((b, s, pool_width), F32),
            jax.ShapeDtypeStruct((b, s, 2 * conv_width), F32),
        ],
        compiler_params=_params(2),
        name="mixer_in",
    )(x, norm_g, w_a, q_g, k_g)


def _attn_kernel(bound_ref, qt_ref, k_ref, vt_ref, lam_ref, g_ref, o_ref,
                 q2_ref, q2b_ref, m_ref, l_ref, acc_ref, *, layer, tq, lam_init):
    first_step = (pl.program_id(0) == 0) & (pl.program_id(1) == 0)
    i = pl.program_id(1)
    heads = qt_ref.shape[1]
    half = tq // 2
    bound = bound_ref[layer]

    @pl.when(first_step)
    def _():
        q2_ref[...] = jnp.zeros_like(q2_ref)
        q2b_ref[...] = jnp.zeros_like(q2b_ref)

    for hd in range(heads):
        q2_ref[hd, :HEAD_DIM, :tq] = qt_ref[0, hd, :HEAD_DIM, :]
        q2_ref[hd, HEAD_DIM:, tq:] = qt_ref[0, hd, HEAD_DIM:, :]
        q2b_ref[hd, :HEAD_DIM, :half] = qt_ref[0, hd, :HEAD_DIM, half:]
        q2b_ref[hd, HEAD_DIM:, half:] = qt_ref[0, hd, HEAD_DIM:, half:]

    def causal(n_keys, n_queries):
        key = lax.broadcasted_iota(jnp.int32, (n_keys, n_queries), 0)
        qry = lax.broadcasted_iota(jnp.int32, (n_keys, n_queries), 1)
        ok = key <= qry
        return jnp.concatenate([ok, ok], axis=1)

    def fixed_offset_path():
        tile0 = pl.multiple_of(i * tq, tq)
        tile1 = pl.multiple_of(i * tq + half, half)

        def qk_a(hd):
            return _dot(k_ref[0, hd, pl.ds(tile0, half), :], q2_ref[hd])

        def use_a(hd, s):
            p = jnp.exp2(jnp.where(causal(half, tq), s, NEG_INF) - bound)
            l_ref[hd] = jnp.sum(p.reshape(half // SUBLANES, SUBLANES, 2 * tq), axis=0)
            acc_ref[hd] = _dot(vt_ref[0, hd, :, pl.ds(tile0, half)], p.astype(BF16))

        _pipelined(qk_a, use_a, heads)

        def qk_b(hd):
            return _dot(k_ref[0, hd, pl.ds(tile1, half), :], q2b_ref[hd])

        def use_b(hd, s):
            p = jnp.exp2(jnp.where(causal(half, half), s, NEG_INF) - bound)
            ls = jnp.sum(p.reshape(half // SUBLANES, SUBLANES, 2 * half), axis=0)
            pv = _dot(vt_ref[0, hd, :, pl.ds(tile1, half)], p.astype(BF16))
            l_ref[hd, :, half:tq] += ls[:, :half]
            l_ref[hd, :, tq + half:] += ls[:, half:]
            acc_ref[hd, :, half:tq] += pv[:, :half]
            acc_ref[hd, :, tq + half:] += pv[:, half:]

        _pipelined(qk_b, use_b, heads)

        def body(j, carry):
            start = pl.multiple_of(j * tq, tq)

            def qk(hd):
                return _dot(k_ref[0, hd, pl.ds(start, tq), :], q2_ref[hd])

            def use(hd, s):
                p = jnp.exp2(s - bound)
                l_ref[hd] += jnp.sum(p.reshape(tq // SUBLANES, SUBLANES, 2 * tq), axis=0)
                acc_ref[hd] += _dot(vt_ref[0, hd, :, pl.ds(start, tq)], p.astype(BF16))

            _pipelined(qk, use, heads)
            return carry

        lax.fori_loop(0, i, body, 0)

    def running_max_path():
        m_ref[...] = jnp.full_like(m_ref, NEG_INF)
        l_ref[...] = jnp.zeros_like(l_ref)
        acc_ref[...] = jnp.zeros_like(acc_ref)

        def block(j, diagonal):
            start = pl.multiple_of(j * tq, tq)
            for hd in range(heads):
                s = _dot(k_ref[0, hd, pl.ds(start, tq), :], q2_ref[hd])
                if diagonal:
                    s = jnp.where(causal(tq, tq), s, NEG_INF)
                m_old = m_ref[hd]
                m_new = jnp.maximum(m_old, jnp.max(s, axis=0, keepdims=True))
                alpha = jnp.exp2(m_old - m_new)
                p = jnp.exp2(s - m_new)
                m_ref[hd] = m_new
                l_ref[hd, 0:1, :] = alpha * l_ref[hd, 0:1, :] + jnp.sum(p, axis=0, keepdims=True)
                acc_ref[hd] = alpha * acc_ref[hd] + _dot(vt_ref[0, hd, :, pl.ds(start, tq)],
                                                         p.astype(BF16))

        def body(j, carry):
            block(j, False)
            return carry

        lax.fori_loop(0, i, body, 0)
        block(i, True)

    fixed_ok = bound <= MAX_FIXED_OFFSET
    pl.when(fixed_ok)(fixed_offset_path)
    pl.when(jnp.logical_not(fixed_ok))(running_max_path)

    lq = lam_ref[...]
    lam = (jnp.exp(jnp.sum(lq[0:1] * lq[1:2], axis=-1, keepdims=True))
           - jnp.exp(jnp.sum(lq[2:3] * lq[3:4], axis=-1, keepdims=True)) + lam_init)
    for hd in range(heads):
        inv_l = 1.0 / jnp.sum(l_ref[hd], axis=0, keepdims=True)
        acc = acc_ref[hd]
        o = acc[:, :tq] * inv_l[:, :tq] - lam * (acc[:, tq:] * inv_l[:, tq:])
        ms = jnp.mean(o * o, axis=0, keepdims=True)
        o = o * lax.rsqrt(ms + EPS) * g_ref[...] * (1.0 - lam_init)
        o_ref[0, :, hd * DA_V_DIM:(hd + 1) * DA_V_DIM] = o.T.astype(BF16)


def _score_bounds(q_g, k_g):
    bound = (HEAD_DIM * HEAD_DIM ** -0.5 * LOG2_E) * jnp.max(jnp.abs(q_g), axis=-1) * jnp.max(jnp.abs(k_g), axis=-1)
    return (1.02 * bound).astype(F32)


def _diff_attention(qt, k, vt, l, bounds, da_lambda, subln_g, tq=512):
    b, heads, _, s = qt.shape
    tq = min(tq, s)
    lam_init = 0.8 - 0.6 * math.exp(-0.3 * l)
    kern = functools.partial(_attn_kernel, layer=l, tq=tq, lam_init=lam_init)
    return pl.pallas_call(
        kern,
        grid=(b, s // tq),
        in_specs=[
            pl.BlockSpec(memory_space=pltpu.SMEM),
            pl.BlockSpec((1, heads, DA_V_DIM, tq), lambda bi, i: (bi, 0, 0, i)),
            pl.BlockSpec((1, heads, s, DA_V_DIM), lambda bi, i: (bi, 0, 0, 0)),
            pl.BlockSpec((1, heads, DA_V_DIM, s), lambda bi, i: (bi, 0, 0, 0)),
            _layer(da_lambda, l),
            _layer(subln_g, l),
        ],
        out_specs=pl.BlockSpec((1, tq, heads * DA_V_DIM), lambda bi, i: (bi, i, 0)),
        out_shape=jax.ShapeDtypeStruct((b, s, heads * DA_V_DIM), BF16),
        scratch_shapes=[pltpu.VMEM((heads, DA_V_DIM, 2 * tq), BF16),
                        pltpu.VMEM((heads, DA_V_DIM, tq), BF16),
                        pltpu.VMEM((heads, 1, 2 * tq), F32),
                        pltpu.VMEM((heads, SUBLANES, 2 * tq), F32),
                        pltpu.VMEM((heads, DA_V_DIM, 2 * tq), F32)],
        compiler_params=_params(2),
        name="diff_attn",
    )(bounds, qt, k, vt, da_lambda, subln_g)


def _mixer_out_kernel(x_ref, oa_ref, zp_ref, zph_ref, zc_ref, zch_ref,
                      g_ref, wg_ref, bg_ref, pa_ref, wbd_ref, ps_ref, pb_ref,
                      dw_ref, db_ref, lng_ref, lnb_ref, pc_ref, wo_ref,
                      o_ref, extp_ref, sums_ref, extc_ref, shift_ref):
    i = pl.program_id(1)
    ts = x_ref.shape[1]
    d = x_ref.shape[2]
    ext = HALO + ts
    x = x_ref[0]
    h = _rms(x, g_ref[...]).astype(BF16)

    def gate_logits(br):
        return _dot(h, wg_ref[:, br * d:(br + 1) * d]) + bg_ref[:, br * d:(br + 1) * d]

    z_a = gate_logits(0)
    y_a = _dot(oa_ref[0], pa_ref[...])

    keep = (i > 0).astype(F32)
    pool_w = zp_ref.shape[-1]
    u = zp_ref[0]
    extp_ref[0:HALO, :] = zph_ref[0] * keep
    extp_ref[HALO:, :] = u
    lane = lax.broadcasted_iota(jnp.int32, (1, pool_w), 1)
    group_w = pool_w // len(POOL_WINDOWS)
    win = jnp.full((1, pool_w), POOL_WINDOWS[-1], jnp.int32)
    for gi in range(len(POOL_WINDOWS) - 2, -1, -1):
        win = jnp.where(lane < (gi + 1) * group_w, POOL_WINDOWS[gi], win)
    prev_ref, lo = extp_ref, 0
    sums = []
    for lvl, w in enumerate(POOL_WINDOWS):
        back = w // 2
        if lvl + 1 < len(POOL_WINDOWS):
            lo += SUBLANES
            sums_ref[lvl, lo:ext, :] = prev_ref[lo:ext, :] + prev_ref[pl.ds(lo - back, ext - lo), :]
            prev_ref = sums_ref.at[lvl]
            sums.append(prev_ref[HALO:ext, :])
        else:
            sums.append(prev_ref[HALO:ext, :] + prev_ref[pl.ds(HALO - back, ts), :])
    wsum = sums[-1]
    for lvl in range(len(POOL_WINDOWS) - 2, -1, -1):
        wsum = jnp.where(win == POOL_WINDOWS[lvl], sums[lvl], wsum)
    pos1 = (i * ts + 1 + lax.broadcasted_iota(jnp.int32, (ts, 1), 0)).astype(F32)
    cnt = jnp.minimum(pos1, win.astype(F32))
    pooled = wsum / cnt - u

    z_b = gate_logits(1)
    merged = _sigmoid(z_a) * y_a
    y_b = _dot(pooled.astype(BF16), wbd_ref[...]) * ps_ref[...]
    y_b = _dot(y_b.astype(BF16), pb_ref[...])

    cw = zc_ref.shape[-1] // 2
    zch = zch_ref[0] * keep
    zc = zc_ref[0]
    extc_ref[0:HALO, :] = zch[:, :cw] * _sigmoid(zch[:, cw:])
    extc_ref[HALO:, :] = zc[:, :cw] * _sigmoid(zc[:, cw:])
    for r in range(1, SUBLANES):
        shift_ref[r - 1] = extc_ref[pl.ds(r, ext - SUBLANES), :]
    conv = jnp.zeros((ts, cw), F32) + db_ref[...]
    base = HALO - (CONV_KERNEL - 1)
    for j in range(CONV_KERNEL):
        q8, r = divmod(base + j, SUBLANES)
        src = extc_ref if r == 0 else shift_ref.at[r - 1]
        conv = conv + src[q8 * SUBLANES:q8 * SUBLANES + ts, :] * dw_ref[j:j + 1, :]
    mu = jnp.mean(conv, axis=-1, keepdims=True)
    xc = conv - mu
    var = jnp.mean(xc * xc, axis=-1, keepdims=True)
    ln = xc * lax.rsqrt(var + EPS) * lng_ref[...] + lnb_ref[...]

    z_c = gate_logits(2)
    merged = merged + _sigmoid(z_b) * y_b
    y_c = _dot((ln * _sigmoid(ln)).astype(BF16), pc_ref[...])
    merged = merged + _sigmoid(z_c) * y_c
    o_ref[0] = x + _dot(merged.astype(BF16), wo_ref[...])


def _mixer_out(x, o_attn, zp, zc, l, norm_g, w_g, b_gate, p_a, w_bd, pool_scale, p_b,
               conv_dw, conv_db, ln_g, ln_b, p_c, w_out, ts=512):
    b, s, d = x.shape
    ts = min(ts, s)
    pool_w = zp.shape[-1]
    cw = zc.shape[-1] // 2
    aw = o_attn.shape[-1]
    per = ts // HALO
    cur = lambda bi, i: (bi, i, 0)
    prev = lambda bi, i: (bi, jnp.maximum(i * per - 1, 0), 0)
    params = (norm_g, w_g, b_gate, p_a, w_bd, pool_scale, p_b,
              conv_dw, conv_db, ln_g, ln_b, p_c, w_out)
    return pl.pallas_call(
        _mixer_out_kernel,
        grid=(b, s // ts),
        in_specs=[
            pl.BlockSpec((1, ts, d), cur),
            pl.BlockSpec((1, ts, aw), cur),
            pl.BlockSpec((1, ts, pool_w), cur),
            pl.BlockSpec((1, HALO, pool_w), prev),
            pl.BlockSpec((1, ts, 2 * cw), cur),
            pl.BlockSpec((1, HALO, 2 * cw), prev),
        ] + [_layer(p, l) for p in params],
        out_specs=pl.BlockSpec((1, ts, d), cur),
        out_shape=jax.ShapeDtypeStruct((b, s, d), F32),
        scratch_shapes=[pltpu.VMEM((HALO + ts, pool_w), F32),
                        pltpu.VMEM((len(POOL_WINDOWS) - 1, HALO + ts, pool_w), F32),
                        pltpu.VMEM((HALO + ts, cw), F32),
                        pltpu.VMEM((SUBLANES - 1, HALO + ts - SUBLANES, cw), F32)],
        compiler_params=_params(2),
        name="mixer_out",
    )(x, o_attn, zp, zp, zc, zc, *params)


def _mem_kv_kernel(mem_ref, g_ref, wkv_ref, kg_ref, kt_ref, v_ref):
    m = _rms(mem_ref[0], g_ref[...]).astype(BF16)
    kv = _dot(m, wkv_ref[...])
    xw = kv.shape[-1] // 2
    k = _group64_rms(kv[:, :xw], kg_ref[...])
    kt_ref[...] = k.T.astype(BF16)
    v_ref[...] = kv[:, xw:].astype(BF16)


def _mem_kv(mem, norm_g, w_kv, k_g):
    b, n_mem, d = mem.shape
    depth = w_kv.shape[0]
    xw = w_kv.shape[-1] // 2
    per_layer = lambda a: pl.BlockSpec((None,) + a.shape[1:], lambda li, bi: (li,) + (0,) * (a.ndim - 1))
    return pl.pallas_call(
        _mem_kv_kernel,
        grid=(depth, b),
        in_specs=[
            pl.BlockSpec((1, n_mem, d), lambda li, bi: (bi, 0, 0)),
            per_layer(norm_g),
            per_layer(w_kv),
            per_layer(k_g),
        ],
        out_specs=[
            pl.BlockSpec((None, None, xw, n_mem), lambda li, bi: (li, bi, 0, 0)),
            pl.BlockSpec((None, None, n_mem, xw), lambda li, bi: (li, bi, 0, 0)),
        ],
        out_shape=[
            jax.ShapeDtypeStruct((depth, b, xw, n_mem), BF16),
            jax.ShapeDtypeStruct((depth, b, n_mem, xw), BF16),
        ],
        compiler_params=_params(2),
        name="mem_kv",
    )(mem, norm_g, w_kv, k_g)


def _xattn_kernel(x_ref, g_ref, wq_ref, qg_ref, kt_ref, v_ref, wo_ref, o_ref):
    x = x_ref[0]
    h = _rms(x, g_ref[...]).astype(BF16)
    q = _group64_rms(_dot(h, wq_ref[...]), qg_ref[...]) * (HEAD_DIM ** -0.5)
    xw = q.shape[-1]
    lane = lax.broadcasted_iota(jnp.int32, (1, xw), 1)
    kt = kt_ref[...]
    v = v_ref[...]
    o = jnp.zeros(q.shape, F32)
    for hd in range(XA_HEADS):
        sel = (lane >= hd * HEAD_DIM) & (lane < (hd + 1) * HEAD_DIM)
        s = _dot(jnp.where(sel, q, 0.0).astype(BF16), kt)
        p = jnp.exp(s - jnp.max(s, axis=-1, keepdims=True))
        inv_l = 1.0 / jnp.sum(p, axis=-1, keepdims=True)
        o = o + jnp.where(sel, _dot(p.astype(BF16), v) * inv_l, 0.0)
    o_ref[0] = x + _dot(o.astype(BF16), wo_ref[...])


def _xattn(x, kt, v, l, norm_g, w_q, q_g, w_o, ts=512):
    b, s, d = x.shape
    ts = min(ts, s)
    xw, n_mem = kt.shape[2], kt.shape[3]
    return pl.pallas_call(
        _xattn_kernel,
        grid=(b, s // ts),
        in_specs=[
            pl.BlockSpec((1, ts, d), lambda bi, i: (bi, i, 0)),
            _layer(norm_g, l),
            _layer(w_q, l),
            _layer(q_g, l),
            pl.BlockSpec((None, None, xw, n_mem), lambda bi, i: (l, bi, 0, 0)),
            pl.BlockSpec((None, None, n_mem, xw), lambda bi, i: (l, bi, 0, 0)),
            _layer(w_o, l),
        ],
        out_specs=pl.BlockSpec((1, ts, d), lambda bi, i: (bi, i, 0)),
        out_shape=jax.ShapeDtypeStruct((b, s, d), F32),
        compiler_params=_params(2),
        name="xattn",
    )(x, norm_g, w_q, q_g, kt, v, w_o)


def _block_diag(w):
    n, g, c, _ = w.shape
    eye = jnp.eye(g, dtype=w.dtype)
    return (eye[None, :, None, :, None] * w[:, :, :, None, :]).reshape(n, g * c, g * c)


def _tile_heads(g, width):
    return _rows(jnp.tile(g, (1, width // HEAD_DIM)))


def kernel(x, mem, ffn1_norm, ffn1_w_gu, ffn1_w_down, mix_norm, w_in, b_gate, da_q_norm, da_k_norm, da_lambda, da_subln, w_proj_attn, pool_w, pool_scale, w_proj_pool, conv_dw, conv_db, conv_ln_g, conv_ln_b, w_proj_conv, w_out, xa_norm, xa_mem_norm, xa_w_q, xa_w_kv, xa_q_norm, xa_k_norm, xa_w_o, ffn2_norm, ffn2_w_gu, ffn2_w_down):
    b, s, d = x.shape
    depth = w_in.shape[0]
    pool_width = pool_scale.shape[-1]
    conv_width = conv_db.shape[-1]
    w_qk = DA_HEADS * 2 * HEAD_DIM
    xa_width = xa_w_q.shape[-1]
    n_branch_in = 3 * DA_HEADS * DA_V_DIM + pool_width + 2 * conv_width
    bf = lambda a: a.astype(BF16)

    ffn1 = (_rows(ffn1_norm), bf(ffn1_w_gu), bf(ffn1_w_down))
    ffn2 = (_rows(ffn2_norm), bf(ffn2_w_gu), bf(ffn2_w_down))
    mix_g = _rows(mix_norm)
    w_a = bf(w_in[:, :, :n_branch_in])
    w_g = bf(w_in[:, :, n_branch_in:])
    q_g, k_g = _tile_heads(da_q_norm, w_qk), _tile_heads(da_k_norm, w_qk)
    bounds = _score_bounds(da_q_norm, da_k_norm)
    subln = da_subln.reshape(depth, DA_V_DIM, 1)
    mix_out = (mix_g, w_g, _rows(b_gate), bf(w_proj_attn), bf(_block_diag(pool_w)), _rows(pool_scale),
               bf(w_proj_pool), conv_dw, _rows(conv_db), _rows(conv_ln_g), _rows(conv_ln_b),
               bf(w_proj_conv), bf(w_out))
    xa = (_rows(xa_norm), bf(xa_w_q), _tile_heads(xa_q_norm, xa_width), bf(xa_w_o))

    mem_kt, mem_v = _mem_kv(mem, _rows(xa_mem_norm), bf(xa_w_kv), _tile_heads(xa_k_norm, xa_width))

    for l in range(depth):
        x = _ffn(x.reshape(b * s, d), l, *ffn1).reshape(b, s, d)
        qt, k, vt, zp, zc = _mixer_in(x, l, mix_g, w_a, q_g, k_g, pool_width, conv_width)
        o_attn = _diff_attention(qt, k, vt, l, bounds, da_lambda, subln)
        x = _mixer_out(x, o_attn, zp, zc, l, *mix_out)
        x = _xattn(x, mem_kt, mem_v, l, *xa)
        x = _ffn(x.reshape(b * s, d), l, *ffn2).reshape(b, s, d)
    return x
```

```python
import functools
import math

import jax
import jax.numpy as jnp
from jax import lax
from jax.experimental import pallas as pl
from jax.experimental.pallas import tpu as pltpu

F32 = jnp.float32
BF16 = jnp.bfloat16

EPS = 1e-6
NEG_INF = -1e30
LOG2_E = math.log2(math.e)

HEAD_DIM = 64
DA_HEADS = 4
DA_V_DIM = 2 * HEAD_DIM
XA_HEADS = 4
POOL_WINDOWS = (2, 4, 8, 16)
CONV_KERNEL = 31
HALO = 32
LANES = 128
SUBLANES = 8
MAX_FIXED_OFFSET = 56.0

assert POOL_WINDOWS[0] == 2 and all(b == 2 * a for a, b in zip(POOL_WINDOWS, POOL_WINDOWS[1:]))
assert HALO >= SUBLANES * len(POOL_WINDOWS) and HALO >= CONV_KERNEL - 1 and HALO % SUBLANES == 0

VMEM_LIMIT = 56 * 1024 * 1024


def _params(n_grid_dims):
    return pltpu.CompilerParams(
        dimension_semantics=("arbitrary",) * n_grid_dims,
        vmem_limit_bytes=VMEM_LIMIT)


def _layer(arr, l):
    zeros = (0,) * (arr.ndim - 1)
    return pl.BlockSpec((None,) + arr.shape[1:], lambda *_: (l,) + zeros,
                        pipeline_mode=pl.Buffered(1))


def _rows(a):
    return a.reshape(a.shape[0], 1, -1)


def _rms(xf, g):
    ms = jnp.mean(xf * xf, axis=-1, keepdims=True)
    return xf * lax.rsqrt(ms + EPS) * g


def _group64_rms(z, g):
    width = z.shape[-1]
    lo = lax.broadcasted_iota(jnp.int32, (1, LANES), 1) < HEAD_DIM
    outs = []
    for c in range(width // LANES):
        zc = z[:, c * LANES:(c + 1) * LANES]
        sq = zc * zc
        s_lo = jnp.sum(jnp.where(lo, sq, 0.0), axis=-1, keepdims=True)
        s_hi = jnp.sum(jnp.where(lo, 0.0, sq), axis=-1, keepdims=True)
        ms = jnp.where(lo, s_lo, s_hi) * (1.0 / HEAD_DIM)
        outs.append(zc * lax.rsqrt(ms + EPS))
    return jnp.concatenate(outs, axis=-1) * g


def _sigmoid(x):
    return 0.5 * jnp.tanh(0.5 * x) + 0.5


def _dot(a, b):
    return jnp.dot(a, b, preferred_element_type=F32)


def _pipelined(produce, consume, n):
    nxt = produce(0)
    for t in range(n):
        cur = nxt
        if t + 1 < n:
            nxt = produce(t + 1)
        consume(t, cur)


def _ffn_kernel(x_ref, g_ref, wgu_ref, wd_ref, o_ref, *, rows_per_pass):
    d_ff = wd_ref.shape[0]
    for r0 in range(0, x_ref.shape[0], rows_per_pass):
        rows = slice(r0, r0 + rows_per_pass)
        x = x_ref[rows, :]
        h = _rms(x, g_ref[...]).astype(BF16)
        gu = _dot(h, wgu_ref[...])
        gate = gu[:, :d_ff]
        up = gu[:, d_ff:]
        act = (gate * _sigmoid(gate) * up).astype(BF16)
        o_ref[rows, :] = x + 0.5 * _dot(act, wd_ref[...])


def _ffn(x2, l, norm_g, w_gu, w_down, tm=1024, rows_per_pass=512):
    t, d = x2.shape
    tm = min(tm, t)
    rows_per_pass = min(rows_per_pass, tm)
    return pl.pallas_call(
        functools.partial(_ffn_kernel, rows_per_pass=rows_per_pass),
        grid=(t // tm,),
        in_specs=[
            pl.BlockSpec((tm, d), lambda i: (i, 0)),
            _layer(norm_g, l),
            _layer(w_gu, l),
            _layer(w_down, l),
        ],
        out_specs=pl.BlockSpec((tm, d), lambda i: (i, 0)),
        out_shape=jax.ShapeDtypeStruct((t, d), F32),
        compiler_params=_params(1),
        name="ffn",
    )(x2, norm_g, w_gu, w_down)


def _mixer_in_kernel(x_ref, g_ref, w_ref, qg_ref, kg_ref,
                     qt_ref, k_ref, vt_ref, zp_ref, zc_ref, *, rows_per_pass):
    w_qk = DA_HEADS * 2 * HEAD_DIM
    w_v = DA_HEADS * DA_V_DIM
    pool_w = zp_ref.shape[-1]
    off = 2 * w_qk + w_v
    for r0 in range(0, x_ref.shape[1], rows_per_pass):
        rows = slice(r0, r0 + rows_per_pass)
        h = _rms(x_ref[0, rows, :], g_ref[...]).astype(BF16)
        z = _dot(h, w_ref[...])
        q = _group64_rms(z[:, :w_qk], qg_ref[...]) * (HEAD_DIM ** -0.5 * LOG2_E)
        k = _group64_rms(z[:, w_qk:2 * w_qk], kg_ref[...])
        v = z[:, 2 * w_qk:2 * w_qk + w_v]
        for hd in range(DA_HEADS):
            sl = slice(hd * DA_V_DIM, (hd + 1) * DA_V_DIM)
            qt_ref[0, hd, :, rows] = q[:, sl].T.astype(BF16)
            k_ref[0, hd, rows, :] = k[:, sl].astype(BF16)
            vt_ref[0, hd, :, rows] = v[:, sl].T.astype(BF16)
        zp_ref[0, rows, :] = z[:, off:off + pool_w]
        zc_ref[0, rows, :] = z[:, off + pool_w:]


def _mixer_in(x, l, norm_g, w_a, q_g, k_g, pool_width, conv_width, tm=1024, rows_per_pass=512):
    b, s, d = x.shape
    tm = min(tm, s)
    rows_per_pass = min(rows_per_pass, tm)
    return pl.pallas_call(
        functools.partial(_mixer_in_kernel, rows_per_pass=rows_per_pass),
        grid=(b, s // tm),
        in_specs=[
            pl.BlockSpec((1, tm, d), lambda bi, i: (bi, i, 0)),
            _layer(norm_g, l),
            _layer(w_a, l),
            _layer(q_g, l),
            _layer(k_g, l),
        ],
        out_specs=[
            pl.BlockSpec((1, DA_HEADS, DA_V_DIM, tm), lambda bi, i: (bi, 0, 0, i)),
            pl.BlockSpec((1, DA_HEADS, tm, DA_V_DIM), lambda bi, i: (bi, 0, i, 0)),
            pl.BlockSpec((1, DA_HEADS, DA_V_DIM, tm), lambda bi, i: (bi, 0, 0, i)),
            pl.BlockSpec((1, tm, pool_width), lambda bi, i: (bi, i, 0)),
            pl.BlockSpec((1, tm, 2 * conv_width), lambda bi, i: (bi, i, 0)),
        ],
        out_shape=[
            jax.ShapeDtypeStruct((b, DA_HEADS, DA_V_DIM, s), BF16),
            jax.ShapeDtypeStruct((b, DA_HEADS, s, DA_V_DIM), BF16),
            jax.ShapeDtypeStruct((b, DA_HEADS, DA_V_DIM, s), BF16),
            jax.ShapeDtypeStruct((b, s, pool_width), F32),
            jax.ShapeDtypeStruct((b, s, 2 * conv_width), F32),
        ],
        compiler_params=_params(2),
        name="mixer_in",
    )(x, norm_g, w_a, q_g, k_g)


def _attn_kernel(bound_ref, qt_ref, k_ref, vt_ref, lam_ref, g_ref, o_ref,
                 m_ref, l_ref, acc_ref, *q2_refs, layer, tq, lam_init):
    first_step = (pl.program_id(0) == 0) & (pl.program_id(1) == 0)
    i = pl.program_id(1)
    heads = qt_ref.shape[1]
    n_sub = len(q2_refs)
    sk = tq // n_sub
    bound = bound_ref[layer]
    q2_ref = q2_refs[0]

    @pl.when(first_step)
    def _():
        for ref in q2_refs:
            ref[...] = jnp.zeros_like(ref)

    for r, ref in enumerate(q2_refs):
        nq = tq - r * sk
        for hd in range(heads):
            ref[hd, :HEAD_DIM, :nq] = qt_ref[0, hd, :HEAD_DIM, r * sk:]
            ref[hd, HEAD_DIM:, nq:] = qt_ref[0, hd, HEAD_DIM:, r * sk:]

    def causal(n_keys, n_queries):
        key = lax.broadcasted_iota(jnp.int32, (n_keys, n_queries), 0)
        qry = lax.broadcasted_iota(jnp.int32, (n_keys, n_queries), 1)
        ok = key <= qry
        return jnp.concatenate([ok, ok], axis=1)

    def fixed_offset_path():
        for r, ref in enumerate(q2_refs):
            nq = tq - r * sk
            start = pl.multiple_of(i * tq + r * sk, sk)

            def qk_diag(hd, ref=ref, start=start):
                return _dot(k_ref[0, hd, pl.ds(start, sk), :], ref[hd])

            def use_diag(hd, s, r=r, nq=nq, start=start):
                p = jnp.exp2(jnp.where(causal(sk, nq), s, NEG_INF) - bound)
                ls = jnp.sum(p.reshape(sk // SUBLANES, SUBLANES, 2 * nq), axis=0)
                pv = _dot(vt_ref[0, hd, :, pl.ds(start, sk)], p.astype(BF16))
                if r == 0:
                    l_ref[hd] = ls
                    acc_ref[hd] = pv
                else:
                    l_ref[hd, :, r * sk:tq] += ls[:, :nq]
                    l_ref[hd, :, tq + r * sk:] += ls[:, nq:]
                    acc_ref[hd, :, r * sk:tq] += pv[:, :nq]
                    acc_ref[hd, :, tq + r * sk:] += pv[:, nq:]

            _pipelined(qk_diag, use_diag, heads)

        def body(j, carry):
            start = pl.multiple_of(j * tq, tq)

            def qk(hd):
                return _dot(k_ref[0, hd, pl.ds(start, tq), :], q2_ref[hd])

            def use(hd, s):
                p = jnp.exp2(s - bound)
                l_ref[hd] += jnp.sum(p.reshape(tq // SUBLANES, SUBLANES, 2 * tq), axis=0)
                acc_ref[hd] += _dot(vt_ref[0, hd, :, pl.ds(start, tq)], p.astype(BF16))

            _pipelined(qk, use, heads)
            return carry

        lax.fori_loop(0, i, body, 0)

    def running_max_path():
        m_ref[...] = jnp.full_like(m_ref, NEG_INF)
        l_ref[...] = jnp.zeros_like(l_ref)
        acc_ref[...] = jnp.zeros_like(acc_ref)

        def block(j, diagonal):
            start = pl.multiple_of(j * tq, tq)
            for hd in range(heads):
                s = _dot(k_ref[0, hd, pl.ds(start, tq), :], q2_ref[hd])
                if diagonal:
                    s = jnp.where(causal(tq, tq), s, NEG_INF)
                m_old = m_ref[hd]
                m_new = jnp.maximum(m_old, jnp.max(s, axis=0, keepdims=True))
                alpha = jnp.exp2(m_old - m_new)
                p = jnp.exp2(s - m_new)
                m_ref[hd] = m_new
                l_ref[hd, 0:1, :] = alpha * l_ref[hd, 0:1, :] + jnp.sum(p, axis=0, keepdims=True)
                acc_ref[hd] = alpha * acc_ref[hd] + _dot(vt_ref[0, hd, :, pl.ds(start, tq)],
                                                         p.astype(BF16))

        def body(j, carry):
            block(j, False)
            return carry

        lax.fori_loop(0, i, body, 0)
        block(i, True)

    fixed_ok = bound <= MAX_FIXED_OFFSET
    pl.when(fixed_ok)(fixed_offset_path)
    pl.when(jnp.logical_not(fixed_ok))(running_max_path)

    lq = lam_ref[...]
    lam = (jnp.exp(jnp.sum(lq[0:1] * lq[1:2], axis=-1, keepdims=True))
           - jnp.exp(jnp.sum(lq[2:3] * lq[3:4], axis=-1, keepdims=True)) + lam_init)
    for hd in range(heads):
        inv_l = 1.0 / jnp.sum(l_ref[hd], axis=0, keepdims=True)
        acc = acc_ref[hd]
        o = acc[:, :tq] * inv_l[:, :tq] - lam * (acc[:, tq:] * inv_l[:, tq:])
        ms = jnp.mean(o * o, axis=0, keepdims=True)
        o = o * lax.rsqrt(ms + EPS) * g_ref[...] * (1.0 - lam_init)
        o_ref[0, :, hd * DA_V_DIM:(hd + 1) * DA_V_DIM] = o.T.astype(BF16)


def _score_bounds(q_g, k_g):
    bound = (HEAD_DIM * HEAD_DIM ** -0.5 * LOG2_E) * jnp.max(jnp.abs(q_g), axis=-1) * jnp.max(jnp.abs(k_g), axis=-1)
    return (1.02 * bound).astype(F32)


def _diff_attention(qt, k, vt, l, bounds, da_lambda, subln_g, tq=1024, n_sub=4):
    b, heads, _, s = qt.shape
    tq = min(tq, s)
    sk = tq // n_sub
    lam_init = 0.8 - 0.6 * math.exp(-0.3 * l)
    kern = functools.partial(_attn_kernel, layer=l, tq=tq, lam_init=lam_init)
    return pl.pallas_call(
        kern,
        grid=(b, s // tq),
        in_specs=[
            pl.BlockSpec(memory_space=pltpu.SMEM),
            pl.BlockSpec((1, heads, DA_V_DIM, tq), lambda bi, i: (bi, 0, 0, i)),
            pl.BlockSpec((1, heads, s, DA_V_DIM), lambda bi, i: (bi, 0, 0, 0)),
            pl.BlockSpec((1, heads, DA_V_DIM, s), lambda bi, i: (bi, 0, 0, 0)),
            _layer(da_lambda, l),
            _layer(subln_g, l),
        ],
        out_specs=pl.BlockSpec((1, tq, heads * DA_V_DIM), lambda bi, i: (bi, i, 0)),
        out_shape=jax.ShapeDtypeStruct((b, s, heads * DA_V_DIM), BF16),
        scratch_shapes=[pltpu.VMEM((heads, 1, 2 * tq), F32),
                        pltpu.VMEM((heads, SUBLANES, 2 * tq), F32),
                        pltpu.VMEM((heads, DA_V_DIM, 2 * tq), F32)]
                       + [pltpu.VMEM((heads, DA_V_DIM, 2 * (tq - r * sk)), BF16) for r in range(n_sub)],
        compiler_params=_params(2),
        name="diff_attn",
    )(bounds, qt, k, vt, da_lambda, subln_g)


def _mixer_out_kernel(x_ref, oa_ref, zp_ref, zph_ref, zc_ref, zch_ref,
                      g_ref, wg_ref, bg_ref, pa_ref, wbd_ref, ps_ref, pb_ref,
                      dw_ref, db_ref, lng_ref, lnb_ref, pc_ref, wo_ref,
                      o_ref, extp_ref, sums_ref, extc_ref, shift_ref):
    i = pl.program_id(1)
    ts = x_ref.shape[1]
    d = x_ref.shape[2]
    ext = HALO + ts
    x = x_ref[0]
    h = _rms(x, g_ref[...]).astype(BF16)

    def gate_logits(br):
        return _dot(h, wg_ref[:, br * d:(br + 1) * d]) + bg_ref[:, br * d:(br + 1) * d]

    z_a = gate_logits(0)
    y_a = _dot(oa_ref[0], pa_ref[...])

    keep = (i > 0).astype(F32)
    pool_w = zp_ref.shape[-1]
    u = zp_ref[0]
    extp_ref[0:HALO, :] = zph_ref[0] * keep
    extp_ref[HALO:, :] = u
    lane = lax.broadcasted_iota(jnp.int32, (1, pool_w), 1)
    group_w = pool_w // len(POOL_WINDOWS)
    win = jnp.full((1, pool_w), POOL_WINDOWS[-1], jnp.int32)
    for gi in range(len(POOL_WINDOWS) - 2, -1, -1):
        win = jnp.where(lane < (gi + 1) * group_w, POOL_WINDOWS[gi], win)
    prev_ref, lo = extp_ref, 0
    sums = []
    for lvl, w in enumerate(POOL_WINDOWS):
        back = w // 2
        if lvl + 1 < len(POOL_WINDOWS):
            lo += SUBLANES
            sums_ref[lvl, lo:ext, :] = prev_ref[lo:ext, :] + prev_ref[pl.ds(lo - back, ext - lo), :]
            prev_ref = sums_ref.at[lvl]
            sums.append(prev_ref[HALO:ext, :])
        else:
            sums.append(prev_ref[HALO:ext, :] + prev_ref[pl.ds(HALO - back, ts), :])
    wsum = sums[-1]
    for lvl in range(len(POOL_WINDOWS) - 2, -1, -1):
        wsum = jnp.where(win == POOL_WINDOWS[lvl], sums[lvl], wsum)
    pos1 = (i * ts + 1 + lax.broadcasted_iota(jnp.int32, (ts, 1), 0)).astype(F32)
    cnt = jnp.minimum(pos1, win.astype(F32))
    pooled = wsum / cnt - u

    z_b = gate_logits(1)
    merged = _sigmoid(z_a) * y_a
    y_b = _dot(pooled.astype(BF16), wbd_ref[...]) * ps_ref[...]
    y_b = _dot(y_b.astype(BF16), pb_ref[...])

    cw = zc_ref.shape[-1] // 2
    zch = zch_ref[0] * keep
    zc = zc_ref[0]
    extc_ref[0:HALO, :] = zch[:, :cw] * _sigmoid(zch[:, cw:])
    extc_ref[HALO:, :] = zc[:, :cw] * _sigmoid(zc[:, cw:])
    for r in range(1, SUBLANES):
        shift_ref[r - 1] = extc_ref[pl.ds(r, ext - SUBLANES), :]
    conv = jnp.zeros((ts, cw), F32) + db_ref[...]
    base = HALO - (CONV_KERNEL - 1)
    for j in range(CONV_KERNEL):
        q8, r = divmod(base + j, SUBLANES)
        src = extc_ref if r == 0 else shift_ref.at[r - 1]
        conv = conv + src[q8 * SUBLANES:q8 * SUBLANES + ts, :] * dw_ref[j:j + 1, :]
    mu = jnp.mean(conv, axis=-1, keepdims=True)
    xc = conv - mu
    var = jnp.mean(xc * xc, axis=-1, keepdims=True)
    ln = xc * lax.rsqrt(var + EPS) * lng_ref[...] + lnb_ref[...]

    z_c = gate_logits(2)
    merged = merged + _sigmoid(z_b) * y_b
    y_c = _dot((ln * _sigmoid(ln)).astype(BF16), pc_ref[...])
    merged = merged + _sigmoid(z_c) * y_c
    o_ref[0] = x + _dot(merged.astype(BF16), wo_ref[...])


def _mixer_out(x, o_attn, zp, zc, l, norm_g, w_g, b_gate, p_a, w_bd, pool_scale, p_b,
               conv_dw, conv_db, ln_g, ln_b, p_c, w_out, ts=512):
    b, s, d = x.shape
    ts = min(ts, s)
    pool_w = zp.shape[-1]
    cw = zc.shape[-1] // 2
    aw = o_attn.shape[-1]
    per = ts // HALO
    cur = lambda bi, i: (bi, i, 0)
    prev = lambda bi, i: (bi, jnp.maximum(i * per - 1, 0), 0)
    params = (norm_g, w_g, b_gate, p_a, w_bd, pool_scale, p_b,
              conv_dw, conv_db, ln_g, ln_b, p_c, w_out)
    return pl.pallas_call(
        _mixer_out_kernel,
        grid=(b, s // ts),
        in_specs=[
            pl.BlockSpec((1, ts, d), cur),
            pl.BlockSpec((1, ts, aw), cur),
            pl.BlockSpec((1, ts, pool_w), cur),
            pl.BlockSpec((1, HALO, pool_w), prev),
            pl.BlockSpec((1, ts, 2 * cw), cur),
            pl.BlockSpec((1, HALO, 2 * cw), prev),
        ] + [_layer(p, l) for p in params],
        out_specs=pl.BlockSpec((1, ts, d), cur),
        out_shape=jax.ShapeDtypeStruct((b, s, d), F32),
        scratch_shapes=[pltpu.VMEM((HALO + ts, pool_w), F32),
                        pltpu.VMEM((len(POOL_WINDOWS) - 1, HALO + ts, pool_w), F32),
                        pltpu.VMEM((HALO + ts, cw), F32),
                        pltpu.VMEM((SUBLANES - 1, HALO + ts - SUBLANES, cw), F32)],
        compiler_params=_params(2),
        name="mixer_out",
    )(x, o_attn, zp, zp, zc, zc, *params)


def _mem_kv_kernel(mem_ref, g_ref, wkv_ref, kg_ref, kt_ref, v_ref):
    m = _rms(mem_ref[0], g_ref[...]).astype(BF16)
    kv = _dot(m, wkv_ref[...])
    xw = kv.shape[-1] // 2
    k = _group64_rms(kv[:, :xw], kg_ref[...])
    kt_ref[...] = k.T.astype(BF16)
    v_ref[...] = kv[:, xw:].astype(BF16)


def _mem_kv(mem, norm_g, w_kv, k_g):
    b, n_mem, d = mem.shape
    depth = w_kv.shape[0]
    xw = w_kv.shape[-1] // 2
    per_layer = lambda a: pl.BlockSpec((None,) + a.shape[1:], lambda li, bi: (li,) + (0,) * (a.ndim - 1))
    return pl.pallas_call(
        _mem_kv_kernel,
        grid=(depth, b),
        in_specs=[
            pl.BlockSpec((1, n_mem, d), lambda li, bi: (bi, 0, 0)),
            per_layer(norm_g),
            per_layer(w_kv),
            per_layer(k_g),
        ],
        out_specs=[
            pl.BlockSpec((None, None, xw, n_mem), lambda li, bi: (li, bi, 0, 0)),
            pl.BlockSpec((None, None, n_mem, xw), lambda li, bi: (li, bi, 0, 0)),
        ],
        out_shape=[
            jax.ShapeDtypeStruct((depth, b, xw, n_mem), BF16),
            jax.ShapeDtypeStruct((depth, b, n_mem, xw), BF16),
        ],
        compiler_params=_params(2),
        name="mem_kv",
    )(mem, norm_g, w_kv, k_g)


def _xattn_kernel(x_ref, g_ref, wq_ref, qg_ref, kt_ref, v_ref, wo_ref, o_ref, *, rows_per_pass):
    xw = wq_ref.shape[-1]
    lane = lax.broadcasted_iota(jnp.int32, (1, xw), 1)
    kt = kt_ref[...]
    v = v_ref[...]

    def head_lanes(hd):
        return (lane >= hd * HEAD_DIM) & (lane < (hd + 1) * HEAD_DIM)

    for r0 in range(0, x_ref.shape[1], rows_per_pass):
        rows = slice(r0, r0 + rows_per_pass)
        x = x_ref[0, rows, :]
        h = _rms(x, g_ref[...]).astype(BF16)
        q = _group64_rms(_dot(h, wq_ref[...]), qg_ref[...]) * (HEAD_DIM ** -0.5)
        outs = []

        def scores(hd):
            return _dot(jnp.where(head_lanes(hd), q, 0.0).astype(BF16), kt)

        def attend(hd, s):
            p = jnp.exp(s - jnp.max(s, axis=-1, keepdims=True))
            inv_l = 1.0 / jnp.sum(p, axis=-1, keepdims=True)
            outs.append(jnp.where(head_lanes(hd), _dot(p.astype(BF16), v) * inv_l, 0.0))

        _pipelined(scores, attend, XA_HEADS)
        o = functools.reduce(lambda a, b: a + b, outs)
        o_ref[0, rows, :] = x + _dot(o.astype(BF16), wo_ref[...])


def _xattn(x, kt, v, l, norm_g, w_q, q_g, w_o, ts=1024, rows_per_pass=512):
    b, s, d = x.shape
    ts = min(ts, s)
    rows_per_pass = min(rows_per_pass, ts)
    xw, n_mem = kt.shape[2], kt.shape[3]
    return pl.pallas_call(
        functools.partial(_xattn_kernel, rows_per_pass=rows_per_pass),
        grid=(b, s // ts),
        in_specs=[
            pl.BlockSpec((1, ts, d), lambda bi, i: (bi, i, 0)),
            _layer(norm_g, l),
            _layer(w_q, l),
            _layer(q_g, l),
            pl.BlockSpec((None, None, xw, n_mem), lambda bi, i: (l, bi, 0, 0)),
            pl.BlockSpec((None, None, n_mem, xw), lambda bi, i: (l, bi, 0, 0)),
            _layer(w_o, l),
        ],
        out_specs=pl.BlockSpec((1, ts, d), lambda bi, i: (bi, i, 0)),
        out_shape=jax.ShapeDtypeStruct((b, s, d), F32),
        compiler_params=_params(2),
        name="xattn",
    )(x, norm_g, w_q, q_g, kt, v, w_o)


def _block_diag(w):
    n, g, c, _ = w.shape
    eye = jnp.eye(g, dtype=w.dtype)
    return (eye[None, :, None, :, None] * w[:, :, :, None, :]).reshape(n, g * c, g * c)


def _tile_heads(g, width):
    return _rows(jnp.tile(g, (1, width // HEAD_DIM)))


def kernel(x, mem, ffn1_norm, ffn1_w_gu, ffn1_w_down, mix_norm, w_in, b_gate, da_q_norm, da_k_norm, da_lambda, da_subln, w_proj_attn, pool_w, pool_scale, w_proj_pool, conv_dw, conv_db, conv_ln_g, conv_ln_b, w_proj_conv, w_out, xa_norm, xa_mem_norm, xa_w_q, xa_w_kv, xa_q_norm, xa_k_norm, xa_w_o, ffn2_norm, ffn2_w_gu, ffn2_w_down):
    b, s, d = x.shape
    depth = w_in.shape[0]
    pool_width = pool_scale.shape[-1]
    conv_width = conv_db.shape[-1]
    w_qk = DA_HEADS * 2 * HEAD_DIM
    xa_width = xa_w_q.shape[-1]
    n_branch_in = 3 * DA_HEADS * DA_V_DIM + pool_width + 2 * conv_width
    bf = lambda a: a.astype(BF16)

    ffn1 = (_rows(ffn1_norm), bf(ffn1_w_gu), bf(ffn1_w_down))
    ffn2 = (_rows(ffn2_norm), bf(ffn2_w_gu), bf(ffn2_w_down))
    mix_g = _rows(mix_norm)
    w_a = bf(w_in[:, :, :n_branch_in])
    w_g = bf(w_in[:, :, n_branch_in:])
    q_g, k_g = _tile_heads(da_q_norm, w_qk), _tile_heads(da_k_norm, w_qk)
    bounds = _score_bounds(da_q_norm, da_k_norm)
    subln = da_subln.reshape(depth, DA_V_DIM, 1)
    mix_out = (mix_g, w_g, _rows(b_gate), bf(w_proj_attn), bf(_block_diag(pool_w)), _rows(pool_scale),
               bf(w_proj_pool), conv_dw, _rows(conv_db), _rows(conv_ln_g), _rows(conv_ln_b),
               bf(w_proj_conv), bf(w_out))
    xa = (_rows(xa_norm), bf(xa_w_q), _tile_heads(xa_q_norm, xa_width), bf(xa_w_o))

    mem_kt, mem_v = _mem_kv(mem, _rows(xa_mem_norm), bf(xa_w_kv), _tile_heads(xa_k_norm, xa_width))

    for l in range(depth):
        x = _ffn(x.reshape(b * s, d), l, *ffn1).reshape(b, s, d)
        qt, k, vt, zp, zc = _mixer_in(x, l, mix_g, w_a, q_g, k_g, pool_width, conv_width)
        o_attn = _diff_attention(qt, k, vt, l, bounds, da_lambda, subln)
        x = _mixer_out(x, o_attn, zp, zc, l, *mix_out)
        x = _xattn(x, mem_kt, mem_v, l, *xa)
        x = _ffn(x.reshape(b * s, d), l, *ffn2).reshape(b, s, d)
    return x
```
